```python
import jax
import jax.numpy as jnp
from jax import lax
import numpy as np


D_MODEL = 1024
BATCH = 2
SEQ = 8192
DEPTH = 2
DEC_BATCH = 16
DEC_SEQ = 4096
PAST_LEN = 128

HEAD_DIM = 64
A_Q_HEADS = 6
A_KV_HEADS = 2
A_GROUP = A_Q_HEADS // A_KV_HEADS
AXIAL_THETA = 10000.0
Q_BLOCK = 128
GRID_W = 64
B_HEADS = 6
DIL_CONFIGS = ((128, 1), (512, 4), (2048, 16))
B_HEADS_PER_CFG = B_HEADS // len(DIL_CONFIGS)
ROPE_THETA = 500000.0
ROT_DIMS = HEAD_DIM // 4
C_HEADS = 6
CONV_K = 5
CHUNK = 64
D_HEADS = 6
N_KEYS = 128
N_EXPERTS = N_KEYS * N_KEYS
PEER_HEADS = 8
PEER_QDIM = 256
PEER_HALF = PEER_QDIM // 2
PEER_TOPK = 16
PEER_BLOCK = 128
EPS = 1e-6

A_WIDTH = A_Q_HEADS * HEAD_DIM
A_KV_WIDTH = A_KV_HEADS * HEAD_DIM
B_WIDTH = B_HEADS * HEAD_DIM
C_WIDTH = C_HEADS * HEAD_DIM
D_WIDTH = D_HEADS * HEAD_DIM
IN_SPLITS = (A_WIDTH, A_KV_WIDTH, A_KV_WIDTH, B_WIDTH, B_WIDTH, B_WIDTH, 3 * C_WIDTH, 2 * C_HEADS, 2 * C_HEADS, C_WIDTH, D_WIDTH, D_WIDTH, 2 * D_WIDTH, D_WIDTH)
IN_WIDTH = sum(IN_SPLITS)
OUT_WIDTH = A_WIDTH + B_HEADS_PER_CFG * HEAD_DIM + C_WIDTH + D_WIDTH

kernel_name = 'hybrid_parallel_heads_peer_encoder'


def _rmsnorm(x, w):
    xf = x.astype(jnp.float32)
    y = xf * lax.rsqrt(jnp.mean(xf * xf, axis=-1, keepdims=True) + EPS)
    return (y * w.astype(jnp.float32)).astype(x.dtype)


def _l2norm(x):
    return x * lax.rsqrt(jnp.sum(x * x, axis=-1, keepdims=True) + EPS)


def _rope(x, pos, theta, rot):
    half = rot // 2
    inv_freq = theta ** (-jnp.arange(half, dtype=jnp.float32) / half)
    ang = pos.astype(jnp.float32)[:, None] * inv_freq[None, :]
    cos = jnp.cos(ang)[:, None, :]
    sin = jnp.sin(ang)[:, None, :]
    xf = x.astype(jnp.float32)
    x1 = xf[..., :half]
    x2 = xf[..., half:rot]
    out = jnp.concatenate([x1 * cos - x2 * sin, x2 * cos + x1 * sin, xf[..., rot:]], axis=-1)
    return out.astype(x.dtype)


def _axial_rope(x, row, col):
    half = HEAD_DIM // 2
    return jnp.concatenate([_rope(x[..., :half], row, AXIAL_THETA, half), _rope(x[..., half:], col, AXIAL_THETA, half)], axis=-1)


def _split_cols(z):
    parts = []
    off = 0
    for width in IN_SPLITS:
        parts.append(z[..., off:off + width])
        off += width
    return parts


def _flip(t):
    return jnp.flip(t, axis=1)


def _to_chunks(x):
    B, S, H = x.shape[:3]
    x = x.reshape(B, S // CHUNK, CHUNK, H, *x.shape[3:])
    return jnp.moveaxis(jnp.moveaxis(x, 1, 0), 3, 2)


def _from_chunks(y):
    n, B, H, C, d = y.shape
    return y.transpose(1, 0, 3, 2, 4).reshape(B, n * C, H, d)


def _axial_gqa(q, k, v, qn_w, kn_w, row, col):
    B, S = q.shape[:2]
    q = _axial_rope(_rmsnorm(q, qn_w), row, col)
    k = _axial_rope(_rmsnorm(k, kn_w), row, col)
    scale = HEAD_DIM ** -0.5
    qb = q.reshape(B, S // Q_BLOCK, Q_BLOCK, A_KV_HEADS, A_GROUP, HEAD_DIM).transpose(1, 0, 2, 3, 4, 5)

    def block(q_blk):
        s = jnp.einsum('bqkgd,bskd->bkgqs', q_blk, k).astype(jnp.float32) * scale
        p = jax.nn.softmax(s, axis=-1).astype(v.dtype)
        return jnp.einsum('bkgqs,bskd->bqkgd', p, v)

    o = lax.map(block, qb)
    return o.transpose(1, 0, 2, 3, 4, 5).reshape(B, S, A_WIDTH)


def _dilated_group(q, k, v, window, dil):
    B, S, H, D = q.shape
    W = window // (2 * dil)
    L = S // dil
    nb = -(-L // W)
    Lp = nb * W

    def to_sub(x):
        x = x.reshape(B, L, dil, H, D).transpose(0, 2, 1, 3, 4)
        return jnp.pad(x, ((0, 0), (0, 0), (0, Lp - L), (0, 0), (0, 0)))

    def banded(x):
        xp = jnp.pad(x, ((0, 0), (0, 0), (W, W), (0, 0), (0, 0)))
        return jnp.concatenate([xp[:, :, i * W:i * W + Lp].reshape(B, dil, nb, W, H, D) for i in range(3)], axis=3)

    qs = to_sub(q).reshape(B, dil, nb, W, H, D)
    ks = banded(to_sub(k))
    vs = banded(to_sub(v))
    rel = (jnp.arange(3 * W)[None, :] - W) - jnp.arange(W)[:, None]
    kpos = jnp.arange(nb)[:, None] * W + jnp.arange(3 * W)[None, :] - W
    mask = (jnp.abs(rel) <= W)[None, :, :] & ((kpos >= 0) & (kpos < L))[:, None, :]
    s = jnp.einsum('brnqhd,brnkhd->brnhqk', qs, ks).astype(jnp.float32) * (HEAD_DIM ** -0.5)
    s = jnp.where(mask[None, None, :, None], s, -jnp.inf)
    lse = jax.nn.logsumexp(s, axis=-1, keepdims=True)
    p = jnp.exp(s - lse).astype(v.dtype)
    o = jnp.einsum('brnhqk,brnkhd->brnqhd', p, vs)

    def from_sub(y):
        y = y.reshape(B, dil, Lp, *y.shape[4:])[:, :, :L]
        y = jnp.swapaxes(y, 1, 2)
        return y.reshape(B, S, *y.shape[3:])

    o = from_sub(o)
    lse = from_sub(jnp.swapaxes(lse[..., 0], 3, 4))
    return o, lse


def _dilated_mixture(q, k, v, pos):
    B, S = q.shape[:2]
    q = _rope(q, pos, ROPE_THETA, ROT_DIMS)
    k = _rope(k, pos, ROPE_THETA, ROT_DIMS)
    outs, lses = [], []
    for g, (window, dil) in enumerate(DIL_CONFIGS):
        hs = slice(g * B_HEADS_PER_CFG, (g + 1) * B_HEADS_PER_CFG)
        o, lse = _dilated_group(q[:, :, hs], k[:, :, hs], v[:, :, hs], window, dil)
        outs.append(o.astype(jnp.float32))
        lses.append(lse)
    wts = jax.nn.softmax(jnp.stack(lses, axis=0), axis=0)
    o = jnp.sum(wts[..., None] * jnp.stack(outs, axis=0), axis=0)
    return o.reshape(B, S, B_HEADS_PER_CFG * HEAD_DIM).astype(q.dtype)


def _gated_delta_chunked(q, k, v, beta, g):
    q, k, v, beta, g = [_to_chunks(t.astype(jnp.float32)) for t in (q, k, v, beta, g)]
    n, B, H, C, dk = q.shape
    dv = v.shape[-1]
    tri = jnp.tril(jnp.ones((C, C), dtype=bool))
    strict = jnp.tril(jnp.ones((C, C), dtype=bool), -1)
    gc = jnp.cumsum(g, axis=-1)
    decay = jnp.exp(jnp.where(tri, gc[..., :, None] - gc[..., None, :], -jnp.inf))
    kb = k * beta[..., None]
    m = jnp.where(strict, jnp.einsum('nbhid,nbhjd->nbhij', kb, k) * decay, 0.0)
    a = m + jnp.eye(C, dtype=jnp.float32)
    rhs = jnp.concatenate([v * beta[..., None], kb * jnp.exp(gc)[..., None]], axis=-1)
    sol = lax.linalg.triangular_solve(a, rhs, left_side=True, lower=True, unit_diagonal=True)
    u, w = sol[..., :dv], sol[..., dv:]
    qk = jnp.einsum('nbhid,nbhjd->nbhij', q, k) * decay

    def step(state, xs):
        qc, kc, uc, wc, gcc, qkc = xs
        v_new = uc - jnp.einsum('bhcd,bhde->bhce', wc, state)
        o = jnp.einsum('bhcd,bhde->bhce', qc * jnp.exp(gcc)[..., None], state) + jnp.einsum('bhij,bhje->bhie', qkc, v_new)
        glast = gcc[..., -1:]
        state = state * jnp.exp(glast)[..., None] + jnp.einsum('bhcd,bhce->bhde', kc * jnp.exp(glast - gcc)[..., None], v_new)
        return state, o

    state0 = jnp.zeros((B, H, dk, dv), jnp.float32)
    _, o = lax.scan(step, state0, (q, k, u, w, gc, qk))
    return _from_chunks(o)


def _gdn_mixer(qkv, beta_logit, a_logit, gate, conv_w, a_log, dt_bias, norm_w):
    B, S, ch = qkv.shape
    w = conv_w.reshape(CONV_K, 1, ch)
    qkv = jax.nn.silu(lax.conv_general_dilated(qkv, w, window_strides=(1,), padding=[(CONV_K // 2, CONV_K // 2)], dimension_numbers=('NWC', 'WIO', 'NWC'), feature_group_count=ch))
    qkv = qkv.astype(jnp.float32)
    q = _l2norm(qkv[..., :C_WIDTH].reshape(B, S, C_HEADS, HEAD_DIM)) * (HEAD_DIM ** -0.5)
    k = _l2norm(qkv[..., C_WIDTH:2 * C_WIDTH].reshape(B, S, C_HEADS, HEAD_DIM))
    v = qkv[..., 2 * C_WIDTH:].reshape(B, S, C_HEADS, HEAD_DIM)
    beta = jax.nn.sigmoid(beta_logit.astype(jnp.float32)).reshape(B, S, 2, C_HEADS)
    g = -jnp.exp(a_log.astype(jnp.float32)) * jax.nn.softplus(a_logit.astype(jnp.float32).reshape(B, S, 2, C_HEADS) + dt_bias.astype(jnp.float32))
    o_f = _gated_delta_chunked(q, k, v, beta[:, :, 0], g[:, :, 0])
    o_b = _flip(_gated_delta_chunked(_flip(q), _flip(k), _flip(v), _flip(beta[:, :, 1]), _flip(g[:, :, 1])))
    o = _rmsnorm(o_f + o_b, norm_w) * jax.nn.silu(gate.astype(jnp.float32).reshape(B, S, C_HEADS, HEAD_DIM))
    return o.reshape(B, S, C_WIDTH).astype(gate.dtype)


def _hgrn2_chunked(q, k, v, logf):
    q, k, v, logf = [_to_chunks(t.astype(jnp.float32)) for t in (q, k, v, logf)]
    n, B, H, C, dk = q.shape
    dv = v.shape[-1]
    tri = jnp.tril(jnp.ones((C, C), dtype=bool))[:, :, None]
    b = jnp.cumsum(logf, axis=-2)

    def step(state, xs):
        qc, kc, vc, bc = xs
        dec = jnp.exp(jnp.where(tri, bc[:, :, :, None, :] - bc[:, :, None, :, :], -jnp.inf))
        att = jnp.einsum('bhic,bhijc->bhij', qc, dec * kc[:, :, None, :, :])
        o = jnp.einsum('bhic,bhce->bhie', qc * jnp.exp(bc), state) + jnp.einsum('bhij,bhje->bhie', att, vc)
        blast = bc[:, :, -1:, :]
        state = state * jnp.swapaxes(jnp.exp(blast), -1, -2) + jnp.einsum('bhjc,bhje->bhce', kc * jnp.exp(blast - bc), vc)
        return state, o

    state0 = jnp.zeros((B, H, dk, dv), jnp.float32)
    _, o = lax.scan(step, state0, (q, k, v, b))
    return _from_chunks(o)


def _hgrn2_mixer(q, i, f_logit, gate, lb_param, layer, norm_w):
    B, S, _ = q.shape
    lb_cum = jnp.cumsum(jax.nn.softmax(lb_param.astype(jnp.float32), axis=0), axis=0)
    lb = (lb_cum[layer] - lb_cum[0]).reshape(2, D_HEADS, HEAD_DIM)
    fl = f_logit.astype(jnp.float32).reshape(B, S, 2, D_HEADS, HEAD_DIM)
    logf = jnp.logaddexp(jnp.log(lb), jnp.log1p(-lb) + jax.nn.log_sigmoid(fl))
    k = (1.0 - lb) * jax.nn.sigmoid(-fl)
    qh = q.astype(jnp.float32).reshape(B, S, D_HEADS, HEAD_DIM)
    vh = i.astype(jnp.float32).reshape(B, S, D_HEADS, HEAD_DIM)
    o_f = _hgrn2_chunked(qh, k[:, :, 0], vh, logf[:, :, 0])
    o_b = _flip(_hgrn2_chunked(_flip(qh), _flip(k[:, :, 1]), _flip(vh), _flip(logf[:, :, 1])))
    o = _rmsnorm(o_f + o_b, norm_w) * jax.nn.silu(gate.astype(jnp.float32).reshape(B, S, D_HEADS, HEAD_DIM))
    return o.reshape(B, S, D_WIDTH).astype(gate.dtype)


def _peer(h, w_query, sub_keys, u_tab, v_tab):
    B, S, D = h.shape
    xb = h.reshape(B * S // PEER_BLOCK, PEER_BLOCK, D)

    def block(xt):
        q = (xt @ w_query).reshape(PEER_BLOCK, PEER_HEADS, 2, PEER_HALF)
        s = jnp.einsum('thpc,hpnc->thpn', q, sub_keys).astype(jnp.float32)
        sv, si = lax.top_k(s, PEER_TOPK)
        cand = (sv[:, :, 0, :, None] + sv[:, :, 1, None, :]).reshape(PEER_BLOCK, PEER_HEADS, PEER_TOPK * PEER_TOPK)
        cidx = (si[:, :, 0, :, None] * N_KEYS + si[:, :, 1, None, :]).reshape(PEER_BLOCK, PEER_HEADS, PEER_TOPK * PEER_TOPK)
        tv, ti = lax.top_k(cand, PEER_TOPK)
        eidx = jnp.take_along_axis(cidx, ti, axis=-1)
        g = jax.nn.softmax(tv, axis=-1)
        u = u_tab[eidx]
        act = jax.nn.gelu(jnp.einsum('td,thkd->thk', xt, u).astype(jnp.float32), approximate=False)
        return jnp.einsum('thk,thkd->td', (g * act).astype(v_tab.dtype), v_tab[eidx])

    return lax.map(block, xb).reshape(B, S, D).astype(h.dtype)


def _layer_mixers(h, l, w_in, a_qnorm_w, a_knorm_w, c_conv_w, c_a_log, c_dt_bias, c_norm_w, d_lb, d_norm_w, w_out, pos, row, col):
    B, S, _ = h.shape
    (a_q, a_k, a_v, b_q, b_k, b_v, c_qkv, c_beta, c_a, c_gate, d_q, d_i, d_f, d_gate) = _split_cols(h @ w_in[l])
    heads = lambda t: t.reshape(B, S, -1, HEAD_DIM)
    o_a = _axial_gqa(heads(a_q), heads(a_k), heads(a_v), a_qnorm_w[l], a_knorm_w[l], row, col)
    o_b = _dilated_mixture(heads(b_q), heads(b_k), heads(b_v), pos)
    o_c = _gdn_mixer(c_qkv, c_beta, c_a, c_gate, c_conv_w[l], c_a_log[l], c_dt_bias[l], c_norm_w[l])
    o_d = _hgrn2_mixer(d_q, d_i, d_f, d_gate, d_lb, l, d_norm_w[l])
    mix = jnp.concatenate([o_a, o_b, o_c, o_d], axis=-1)
    return mix @ w_out[l]


def _trunk(x, norm1_w, w_in, a_qnorm_w, a_knorm_w, c_conv_w, c_a_log, c_dt_bias, c_norm_w, d_lb, d_norm_w, w_out, norm2_w, peer_w_query, peer_sub_keys, peer_u, peer_v, final_norm_w):
    B, S, _ = x.shape
    rows = S // GRID_W
    pos = jnp.arange(S)
    row = jnp.repeat(jnp.arange(rows), GRID_W)
    col = jnp.tile(jnp.arange(GRID_W), rows)
    for l in range(DEPTH):
        h = _rmsnorm(x, norm1_w[l])
        x = x + _layer_mixers(h, l, w_in, a_qnorm_w, a_knorm_w, c_conv_w, c_a_log, c_dt_bias, c_norm_w, d_lb, d_norm_w, w_out, pos, row, col).astype(x.dtype)
        h = _rmsnorm(x, norm2_w[l])
        x = x + _peer(h, peer_w_query[l], peer_sub_keys[l], peer_u[l], peer_v[l])
    return _rmsnorm(x, final_norm_w)


def setup_inputs(seed: int = 0) -> dict:
    key = jax.random.key(seed)
    ks = jax.random.split(key, 19)

    def nrm(k, shape, scale):
        return jax.random.normal(k, shape, jnp.float32) * scale

    return {
        'x_prompt': nrm(ks[0], (BATCH, SEQ, D_MODEL), 1.0),
        'x_sample': nrm(ks[1], (DEC_BATCH, DEC_SEQ, D_MODEL), 1.0),
        'norm1_w': 1.0 + nrm(ks[2], (DEPTH, D_MODEL), 0.02),
        'w_in': nrm(ks[3], (DEPTH, D_MODEL, IN_WIDTH), D_MODEL ** -0.5),
        'a_qnorm_w': 1.0 + nrm(ks[4], (DEPTH, HEAD_DIM), 0.02),
        'a_knorm_w': 1.0 + nrm(ks[5], (DEPTH, HEAD_DIM), 0.02),
        'c_conv_w': nrm(ks[6], (DEPTH, CONV_K, 3 * C_WIDTH), CONV_K ** -0.5),
        'c_a_log': jnp.log(jax.random.uniform(ks[7], (DEPTH, 2, C_HEADS), jnp.float32, 1.0, 16.0)),
        'c_dt_bias': nrm(ks[8], (DEPTH, 2, C_HEADS), 0.5) - 3.0,
        'c_norm_w': 1.0 + nrm(ks[9], (DEPTH, HEAD_DIM), 0.02),
        'd_lb': nrm(ks[10], (DEPTH, 2, D_WIDTH), 1.0),
        'd_norm_w': 1.0 + nrm(ks[11], (DEPTH, HEAD_DIM), 0.02),
        'w_out': nrm(ks[12], (DEPTH, OUT_WIDTH, D_MODEL), OUT_WIDTH ** -0.5),
        'norm2_w': 1.0 + nrm(ks[13], (DEPTH, D_MODEL), 0.02),
        'peer_w_query': nrm(ks[14], (DEPTH, D_MODEL, PEER_HEADS * PEER_QDIM), D_MODEL ** -0.5),
        'peer_sub_keys': nrm(ks[15], (DEPTH, PEER_HEADS, 2, N_KEYS, PEER_HALF), PEER_HALF ** -0.5),
        'peer_u': nrm(ks[16], (DEPTH, N_EXPERTS, D_MODEL), D_MODEL ** -0.5),
        'peer_v': nrm(ks[17], (DEPTH, N_EXPERTS, D_MODEL), (PEER_HEADS * PEER_TOPK) ** -0.5),
        'final_norm_w': 1.0 + nrm(ks[18], (D_MODEL,), 0.02),
    }


def reference(x_prompt, x_sample, norm1_w, w_in, a_qnorm_w, a_knorm_w, c_conv_w, c_a_log, c_dt_bias, c_norm_w, d_lb, d_norm_w, w_out, norm2_w, peer_w_query, peer_sub_keys, peer_u, peer_v, final_norm_w):
    y_prompt = _trunk(x_prompt, norm1_w, w_in, a_qnorm_w, a_knorm_w, c_conv_w, c_a_log, c_dt_bias, c_norm_w, d_lb, d_norm_w, w_out, norm2_w, peer_w_query, peer_sub_keys, peer_u, peer_v, final_norm_w)
    y_sample = _trunk(x_sample, norm1_w, w_in, a_qnorm_w, a_knorm_w, c_conv_w, c_a_log, c_dt_bias, c_norm_w, d_lb, d_norm_w, w_out, norm2_w, peer_w_query, peer_sub_keys, peer_u, peer_v, final_norm_w)
    return (y_prompt, y_sample)
```

```python
import functools

import jax
import jax.numpy as jnp
from jax import lax
from jax.experimental import pallas as pl
from jax.experimental.pallas import tpu as pltpu

F32 = jnp.float32
BF16 = jnp.bfloat16

D_MODEL = 1024
DEPTH = 2
HEAD_DIM = 64
LANES = 128
A_Q_HEADS = 6
A_KV_HEADS = 2
A_GROUP = A_Q_HEADS // A_KV_HEADS
AXIAL_THETA = 10000.0
GRID_W = 64
B_HEADS = 6
DIL_CONFIGS = ((128, 1), (512, 4), (2048, 16))
ROPE_THETA = 500000.0
ROT_DIMS = HEAD_DIM // 4
C_HEADS = 6
CONV_K = 5
CHUNK = 64
D_HEADS = 6
N_KEYS = 128
N_EXPERTS = N_KEYS * N_KEYS
PEER_HEADS = 8
PEER_HALF = 128
PEER_TOPK = 16
EPS = 1e-6
NEG = -1e30

A_WIDTH = A_Q_HEADS * HEAD_DIM
A_KV_WIDTH = A_KV_HEADS * HEAD_DIM
B_WIDTH = B_HEADS * HEAD_DIM
C_WIDTH = C_HEADS * HEAD_DIM
D_WIDTH = D_HEADS * HEAD_DIM
IN_SPLITS = (A_WIDTH, A_KV_WIDTH, A_KV_WIDTH, B_WIDTH, B_WIDTH, B_WIDTH, 3 * C_WIDTH, 2 * C_HEADS, 2 * C_HEADS,
             C_WIDTH, D_WIDTH, D_WIDTH, 2 * D_WIDTH, D_WIDTH)

VMEM_LIMIT_BYTES = 48 * 1024 * 1024


def _cparams(*sem):
    return pltpu.CompilerParams(dimension_semantics=sem, vmem_limit_bytes=VMEM_LIMIT_BYTES)


def _bf(x):
    return x.astype(BF16)


def _dot(a, b):
    return jnp.dot(a, b, preferred_element_type=F32)


def _dot_nt(a, b):
    return lax.dot_general(a, b, (((1,), (1,)), ((), ())), preferred_element_type=F32)


def _dot_tn(a, b):
    return lax.dot_general(a, b, (((0,), (0,)), ((), ())), preferred_element_type=F32)


def _split2(x):
    hi = _bf(x)
    lo = _bf(x - hi.astype(F32))
    return hi, lo


def _head_blockdiag(dtype=BF16):
    r = lax.broadcasted_iota(jnp.int32, (LANES, LANES), 0) // HEAD_DIM
    c = lax.broadcasted_iota(jnp.int32, (LANES, LANES), 1) // HEAD_DIM
    return jnp.where(r == c, 1.0, 0.0).astype(dtype)


def _segsum64(x):
    bd = _head_blockdiag()
    hi, lo = _split2(x)
    return _dot(hi, bd) + _dot(lo, bd)


def _rmsnorm_rows(x, w):
    return x * lax.rsqrt(jnp.mean(x * x, axis=-1, keepdims=True) + EPS) * w


def _rope_slab(y, cos, sneg, spos, shift):
    return y * cos + pltpu.roll(y, LANES - shift, 1) * sneg + pltpu.roll(y, shift, 1) * spos


def _sigmoid(x):
    return 1.0 / (1.0 + jnp.exp(-x))


def _silu(x):
    return x * _sigmoid(x)


def _softplus(x):
    return jnp.maximum(x, 0.0) + jnp.log1p(jnp.exp(-jnp.abs(x)))


def _proj_a_kernel(x_ref, nw_ref, w_ref, qnw_ref, knw_ref, cos_ref, sneg_ref, spos_ref, q_ref, k_ref, v_ref):
    h = _rmsnorm_rows(x_ref[...], nw_ref[...])
    z = _dot(_bf(h), w_ref[...])
    cos, sneg, spos = cos_ref[...], sneg_ref[...], spos_ref[...]

    def norm_rope(zs, nw):
        ms = _segsum64(zs * zs) * (1.0 / HEAD_DIM)
        y = zs * lax.rsqrt(ms + EPS) * nw
        return _rope_slab(y, cos, sneg, spos, HEAD_DIM // 4)

    qnw = qnw_ref[...]
    for s in range(A_Q_HEADS):
        q_ref[:, s * LANES:(s + 1) * LANES] = _bf(norm_rope(z[:, s * LANES:(s + 1) * LANES], qnw) * (HEAD_DIM ** -0.5))
    k_ref[...] = _bf(norm_rope(z[:, 6 * LANES:7 * LANES], knw_ref[...]))
    v_ref[...] = _bf(z[:, 7 * LANES:8 * LANES])


def _proj_b_kernel(x_ref, nw_ref, w_ref, cos_ref, sneg_ref, spos_ref, o_ref):
    h = _rmsnorm_rows(x_ref[...], nw_ref[...])
    z = _dot(_bf(h), w_ref[...])
    cos, sneg, spos = cos_ref[...], sneg_ref[...], spos_ref[...]
    for g in range(len(DIL_CONFIGS)):
        c0 = 3 * g * LANES
        q = _rope_slab(z[:, c0:c0 + LANES], cos, sneg, spos, ROT_DIMS // 2) * (HEAD_DIM ** -0.5)
        k = _rope_slab(z[:, c0 + LANES:c0 + 2 * LANES], cos, sneg, spos, ROT_DIMS // 2)
        o_ref[:, c0:c0 + LANES] = _bf(q)
        o_ref[:, c0 + LANES:c0 + 2 * LANES] = _bf(k)
        o_ref[:, c0 + 2 * LANES:c0 + 3 * LANES] = _bf(z[:, c0 + 2 * LANES:c0 + 3 * LANES])


def _proj_plain_kernel(widths, x_ref, nw_ref, w_ref, *o_refs):
    h = _rmsnorm_rows(x_ref[...], nw_ref[...])
    z = _dot(_bf(h), w_ref[...])
    off = 0
    for width, o_ref in zip(widths, o_refs):
        o_ref[...] = z[:, off:off + width].astype(o_ref.dtype)
        off += width


def _row_spec(tm, width):
    return pl.BlockSpec((tm, width), lambda i: (i, 0))


def _const_spec(shape):
    return pl.BlockSpec(shape, lambda i: (0,) * len(shape))


def _proj_a(x2, nw, w, qnw, knw, tabs, seq, tm=256):
    n = x2.shape[0]
    ns = seq // tm
    tab_spec = pl.BlockSpec((tm, LANES), lambda i: (i % ns, 0))
    return pl.pallas_call(
        _proj_a_kernel,
        grid=(n // tm,),
        in_specs=[_row_spec(tm, D_MODEL), _const_spec((1, D_MODEL)), _const_spec(w.shape), _const_spec((1, LANES)),
                  _const_spec((1, LANES)), tab_spec, tab_spec, tab_spec],
        out_specs=[_row_spec(tm, A_Q_HEADS * LANES), _row_spec(tm, LANES), _row_spec(tm, LANES)],
        out_shape=[jax.ShapeDtypeStruct((n, A_Q_HEADS * LANES), BF16), jax.ShapeDtypeStruct((n, LANES), BF16),
                   jax.ShapeDtypeStruct((n, LANES), BF16)],
        compiler_params=_cparams("parallel"),
        name="proj_a",
    )(x2, nw, w, qnw, knw, *tabs)


def _proj_b(x2, nw, w, tabs, seq, tm=256):
    n = x2.shape[0]
    ns = seq // tm
    tab_spec = pl.BlockSpec((tm, LANES), lambda i: (i % ns, 0))
    return pl.pallas_call(
        _proj_b_kernel,
        grid=(n // tm,),
        in_specs=[_row_spec(tm, D_MODEL), _const_spec((1, D_MODEL)), _const_spec(w.shape), tab_spec, tab_spec, tab_spec],
        out_specs=_row_spec(tm, w.shape[1]),
        out_shape=jax.ShapeDtypeStruct((n, w.shape[1]), BF16),
        compiler_params=_cparams("parallel"),
        name="proj_b",
    )(x2, nw, w, *tabs)


def _proj_plain(x2, nw, w, widths, dtypes, name, tm=256):
    n = x2.shape[0]
    return pl.pallas_call(
        functools.partial(_proj_plain_kernel, widths),
        grid=(n // tm,),
        in_specs=[_row_spec(tm, D_MODEL), _const_spec((1, D_MODEL)), _const_spec(w.shape)],
        out_specs=[_row_spec(tm, wd) for wd in widths],
        out_shape=[jax.ShapeDtypeStruct((n, wd), dt) for wd, dt in zip(widths, dtypes)],
        compiler_params=_cparams("parallel"),
        name=name,
    )(x2, nw, w)


def _lane_is_low():
    return lax.broadcasted_iota(jnp.int32, (1, LANES), 1) < HEAD_DIM


def _attn_a_kernel(tk, q_ref, k_ref, v_ref, o_ref):
    tq = q_ref.shape[1]
    nk = k_ref.shape[1] // tk
    rows = A_GROUP * tq
    per_kv = []
    for kv in range(A_KV_HEADS):
        q = jnp.concatenate([q_ref[0, :, (kv * A_GROUP + g) * LANES:(kv * A_GROUP + g + 1) * LANES]
                             for g in range(A_GROUP)], axis=0)

        def body(c, carry, q=q):
            m, l, acc = carry
            start = pl.multiple_of(c * tk, tk)
            s = _dot_nt(q, k_ref[0, pl.ds(start, tk), :])
            m_new = jnp.maximum(m, jnp.max(s, axis=1, keepdims=True))
            alpha = jnp.exp(m - m_new)
            p = jnp.exp(s - m_new)
            l = alpha * l + jnp.sum(p, axis=1, keepdims=True)
            acc = alpha * acc + _dot(_bf(p), v_ref[0, pl.ds(start, tk), :])
            return m_new, l, acc

        init = (jnp.full((rows, 1), NEG, F32), jnp.zeros((rows, 1), F32), jnp.zeros((rows, LANES), F32))
        _, l, acc = lax.fori_loop(0, nk, body, init)
        per_kv.append(acc / l)
    low = _lane_is_low()
    for j in range(A_Q_HEADS // 2):
        halves = []
        for h in (2 * j, 2 * j + 1):
            kv, g = divmod(h, A_GROUP)
            o = per_kv[kv][g * tq:(g + 1) * tq]
            halves.append(o if kv == h % 2 else pltpu.roll(o, HEAD_DIM, 1))
        o_ref[0, :, j * LANES:(j + 1) * LANES] = jnp.where(low, halves[0], halves[1]).astype(o_ref.dtype)


def _attn_a(q, k, v, tq=256, tk=512):
    b, s, _ = q.shape
    return pl.pallas_call(
        functools.partial(_attn_a_kernel, tk),
        grid=(b, s // tq),
        in_specs=[pl.BlockSpec((1, tq, A_Q_HEADS * LANES), lambda bi, i: (bi, i, 0)),
                  pl.BlockSpec((1, s, LANES), lambda bi, i: (bi, 0, 0)),
                  pl.BlockSpec((1, s, LANES), lambda bi, i: (bi, 0, 0))],
        out_specs=pl.BlockSpec((1, tq, A_WIDTH), lambda bi, i: (bi, i, 0)),
        out_shape=jax.ShapeDtypeStruct((b, s, A_WIDTH), BF16),
        compiler_params=_cparams("parallel", "parallel"),
        name="attn_a",
    )(q, k, v)


B_HALF_WINDOW = 64


def _attn_b_kernel(q_ref, k_ref, v_ref, o_ref, lse_ref):
    tq = q_ref.shape[1]
    length = k_ref.shape[1]
    w = B_HALF_WINDOW
    kwid = tq + 2 * w
    l0 = pl.program_id(2) * tq
    ws = pl.multiple_of(jnp.clip(l0 - w, 0, length - kwid), w)
    kw = k_ref[0, pl.ds(ws, kwid), :]
    vw = v_ref[0, pl.ds(ws, kwid), :]
    qpos = l0 + lax.broadcasted_iota(jnp.int32, (tq, 1), 0)
    kpos = ws + lax.broadcasted_iota(jnp.int32, (1, kwid), 1)
    valid = jnp.abs(kpos - qpos) <= w
    q = q_ref[0]
    low = _lane_is_low()
    outs, lses = [], []
    for j in range(2):
        qj = jnp.where(low if j == 0 else jnp.logical_not(low), q, jnp.zeros_like(q))
        s = jnp.where(valid, _dot_nt(qj, kw), NEG)
        m = jnp.max(s, axis=1, keepdims=True)
        p = jnp.exp(s - m)
        z = jnp.sum(p, axis=1, keepdims=True)
        outs.append(_dot(_bf(p), vw) / z)
        lses.append(jnp.broadcast_to(m + jnp.log(z), (tq, LANES)))
    o_ref[0] = jnp.where(low, outs[0], outs[1])
    lse_ref[0] = jnp.where(low, lses[0], lses[1])


def _attn_b(zb, g, tq=128):
    b, s, width = zb.shape
    dil = DIL_CONFIGS[g][1]
    assert DIL_CONFIGS[g][0] // (2 * dil) == B_HALF_WINDOW
    length = s // dil
    nslab = width // LANES
    z = zb.reshape(b, length, dil * width)
    out = jax.ShapeDtypeStruct((b, length, dil * LANES), F32)
    o, lse = pl.pallas_call(
        _attn_b_kernel,
        grid=(b, dil, length // tq),
        in_specs=[pl.BlockSpec((1, tq, LANES), lambda bi, r, i: (bi, i, r * nslab + 3 * g)),
                  pl.BlockSpec((1, length, LANES), lambda bi, r, i: (bi, 0, r * nslab + 3 * g + 1)),
                  pl.BlockSpec((1, length, LANES), lambda bi, r, i: (bi, 0, r * nslab + 3 * g + 2))],
        out_specs=[pl.BlockSpec((1, tq, LANES), lambda bi, r, i: (bi, i, r))] * 2,
        out_shape=[out, out],
        compiler_params=_cparams("parallel", "parallel", "parallel"),
        name=f"attn_b{g}",
    )(z, z, z)
    return o.reshape(b, s, LANES), lse.reshape(b, s, LANES)


HALO_ROWS = 8


def _dot_hp(a, b):
    a_hi, a_lo = _split2(a)
    b_hi, b_lo = _split2(b)
    return _dot(a_hi, b_hi) + (_dot(a_hi, b_lo) + _dot(a_lo, b_hi))


def _gdn_prep_kernel(x_ref, prev_ref, next_ref, cw_ref, ba_ref, par_ref, qkv_ref, bg_ref, ext_ref):
    t = x_ref.shape[1]
    i = pl.program_id(1)
    last = pl.num_programs(1) - 1
    ext_ref[0:HALO_ROWS] = jnp.where(i > 0, prev_ref[0], 0.0)
    ext_ref[HALO_ROWS:HALO_ROWS + t] = x_ref[0]
    ext_ref[HALO_ROWS + t:] = jnp.where(i < last, next_ref[0], 0.0)
    acc = None
    for k in range(CONV_K):
        term = ext_ref[pl.ds(HALO_ROWS - CONV_K // 2 + k, t), :] * cw_ref[k:k + 1, :]
        acc = term if acc is None else acc + term
    y = _silu(acc)
    for s in range(3 * C_WIDTH // LANES):
        ys = y[:, s * LANES:(s + 1) * LANES]
        if s < 2 * C_WIDTH // LANES:
            ys = ys * lax.rsqrt(_segsum64(ys * ys) + EPS)
            if s < C_WIDTH // LANES:
                ys = ys * (HEAD_DIM ** -0.5)
        qkv_ref[0, :, s * LANES:(s + 1) * LANES] = ys
    z = ba_ref[0]
    lane = lax.broadcasted_iota(jnp.int32, (1, LANES), 1)
    a_log, dt_bias = par_ref[0:1, :], par_ref[1:2, :]
    g = -jnp.exp(a_log) * _softplus(z + dt_bias)
    bg_ref[0] = jnp.where(lane < 2 * C_HEADS, _sigmoid(z), g)


def _gdn_prep(cqkv, cba, conv_w, par, t=256):
    b, s, width = cqkv.shape
    nh = t // HALO_ROWS
    last_halo = s // HALO_ROWS - 1
    return pl.pallas_call(
        _gdn_prep_kernel,
        grid=(b, s // t),
        in_specs=[pl.BlockSpec((1, t, width), lambda bi, i: (bi, i, 0)),
                  pl.BlockSpec((1, HALO_ROWS, width), lambda bi, i: (bi, jnp.maximum(i * nh - 1, 0), 0)),
                  pl.BlockSpec((1, HALO_ROWS, width), lambda bi, i: (bi, jnp.minimum((i + 1) * nh, last_halo), 0)),
                  pl.BlockSpec(conv_w.shape, lambda bi, i: (0, 0)),
                  pl.BlockSpec((1, t, LANES), lambda bi, i: (bi, i, 0)),
                  pl.BlockSpec(par.shape, lambda bi, i: (0, 0))],
        out_specs=[pl.BlockSpec((1, t, width), lambda bi, i: (bi, i, 0)),
                   pl.BlockSpec((1, t, LANES), lambda bi, i: (bi, i, 0))],
        out_shape=[jax.ShapeDtypeStruct((b, s, width), F32), jax.ShapeDtypeStruct((b, s, LANES), F32)],
        scratch_shapes=[pltpu.VMEM((t + 2 * HALO_ROWS, width), F32)],
        compiler_params=_cparams("parallel", "parallel"),
        name="gdn_prep",
    )(cqkv, cqkv, cqkv, conv_w, cba, par)


def _chunk_cumsum(x, rev):
    n = x.shape[0]
    row = lax.broadcasted_iota(jnp.int32, (n, 1), 0)
    sh = 1
    while sh < n:
        if rev:
            x = x + jnp.where(row < n - sh, pltpu.roll(x, n - sh, 0), 0.0)
        else:
            x = x + jnp.where(row >= sh, pltpu.roll(x, sh, 0), 0.0)
        sh *= 2
    return x


def _headnorm_gate(o, nw_ref, gate):
    outs = []
    for s in range(o.shape[1] // LANES):
        os_ = o[:, s * LANES:(s + 1) * LANES]
        ms = _segsum64(os_ * os_) * (1.0 / HEAD_DIM)
        outs.append(os_ * lax.rsqrt(ms + EPS) * nw_ref[...] * _silu(gate[:, s * LANES:(s + 1) * LANES]))
    return outs


def _gdn_kernel(rev, qkv_ref, bg_ref, *rest):
    if rev:
        of_ref, gate_ref, nw_ref, o_ref, state_ref, ob_ref = rest
    else:
        o_ref, state_ref = rest
        ob_ref = o_ref.at[0]
    t = qkv_ref.shape[1]

    @pl.when(pl.program_id(1) == 0)
    def _():
        state_ref[...] = jnp.zeros_like(state_ref)

    ri = lax.broadcasted_iota(jnp.int32, (CHUNK, CHUNK), 0)
    ci = lax.broadcasted_iota(jnp.int32, (CHUNK, CHUNK), 1)
    incl = (ci >= ri) if rev else (ci <= ri)
    strict = (ci > ri) if rev else (ci < ri)
    chunks = range(t // CHUNK)
    for c in (reversed(chunks) if rev else chunks):
        r0 = c * CHUNK
        bg = bg_ref[0, r0:r0 + CHUNK, :]
        gc = _chunk_cumsum(bg, rev)
        gc_t = gc.T
        e_gc = jnp.exp(gc)
        g_last = gc[0:1, :] if rev else gc[CHUNK - 1:CHUNK, :]
        e_last = jnp.exp(g_last)
        e_rel = jnp.exp(g_last - gc)
        for h in range(C_HEADS):
            hb = h + (C_HEADS if rev else 0)
            hg = 2 * C_HEADS + hb
            q = qkv_ref[0, r0:r0 + CHUNK, h * HEAD_DIM:(h + 1) * HEAD_DIM]
            k = qkv_ref[0, r0:r0 + CHUNK, C_WIDTH + h * HEAD_DIM:C_WIDTH + (h + 1) * HEAD_DIM]
            v = qkv_ref[0, r0:r0 + CHUNK, 2 * C_WIDTH + h * HEAD_DIM:2 * C_WIDTH + (h + 1) * HEAD_DIM]
            beta = bg[:, hb:hb + 1]
            decay = jnp.exp(jnp.where(incl, gc[:, hg:hg + 1] - gc_t[hg:hg + 1, :], NEG))
            kb = k * beta
            kq = _dot_nt(_bf(jnp.concatenate([kb, q], axis=0)), _bf(k))
            m = jnp.where(strict, kq[:CHUNK] * decay, 0.0)
            qk = kq[CHUNK:] * decay
            r = jnp.concatenate([v * beta, kb * e_gc[:, hg:hg + 1]], axis=1)
            r = r - _dot_hp(m, r)
            p = m
            for _ in range(5):
                p = _dot_hp(p, p)
                r = r + _dot_hp(p, r)
            u, w = r[:, :HEAD_DIM], r[:, HEAD_DIM:]
            state = state_ref[h]
            v_new = u - _dot(_bf(w), _bf(state))
            o = _dot(_bf(q * e_gc[:, hg:hg + 1]), _bf(state)) + _dot(_bf(qk), _bf(v_new))
            state_ref[h] = state * e_last[:, hg:hg + 1] + _dot_tn(_bf(k * e_rel[:, hg:hg + 1]), _bf(v_new))
            ob_ref[r0:r0 + CHUNK, h * HEAD_DIM:(h + 1) * HEAD_DIM] = o
    if rev:
        outs = _headnorm_gate(of_ref[0] + ob_ref[...], nw_ref, gate_ref[0])
        for s, val in enumerate(outs):
            o_ref[0, :, s * LANES:(s + 1) * LANES] = val.astype(o_ref.dtype)


def _gdn_scan(qkv, bg, rev, extra=(), t=256):
    b, s, width = qkv.shape
    n = s // t
    idx = (lambda bi, i: (bi, n - 1 - i, 0)) if rev else (lambda bi, i: (bi, i, 0))
    in_specs = [pl.BlockSpec((1, t, width), idx), pl.BlockSpec((1, t, LANES), idx)]
    scratch = [pltpu.VMEM((C_HEADS, HEAD_DIM, HEAD_DIM), F32)]
    if rev:
        in_specs += [pl.BlockSpec((1, t, C_WIDTH), idx), pl.BlockSpec((1, t, C_WIDTH), idx),
                     pl.BlockSpec((1, LANES), lambda bi, i: (0, 0))]
        scratch.append(pltpu.VMEM((t, C_WIDTH), F32))
    return pl.pallas_call(
        functools.partial(_gdn_kernel, rev),
        grid=(b, n),
        in_specs=in_specs,
        out_specs=pl.BlockSpec((1, t, C_WIDTH), idx),
        out_shape=jax.ShapeDtypeStruct((b, s, C_WIDTH), BF16 if rev else F32),
        scratch_shapes=scratch,
        compiler_params=_cparams("parallel", "arbitrary"),
        name="gdn_bwd" if rev else "gdn_fwd",
    )(qkv, bg, *extra)


def _gdn_mixer(cqkv, cba, gate, conv_w, a_log, dt_bias, norm_w):
    cw = jnp.zeros((HALO_ROWS, cqkv.shape[-1]), F32).at[:CONV_K].set(conv_w)
    par = jnp.zeros((HALO_ROWS, LANES), F32)
    par = par.at[0, 2 * C_HEADS:4 * C_HEADS].set(a_log.reshape(-1)).at[1, 2 * C_HEADS:4 * C_HEADS].set(dt_bias.reshape(-1))
    qkv, bg = _gdn_prep(cqkv, cba, cw, par)
    o_f = _gdn_scan(qkv, bg, False)
    return _gdn_scan(qkv, bg, True, (o_f, gate, jnp.tile(norm_w, 2)[None]))


SUB = 16


def _segment_cumsum(x, seg, rev):
    n = x.shape[0]
    pos = lax.broadcasted_iota(jnp.int32, (n, 1), 0) % seg
    sh = 1
    while sh < seg:
        if rev:
            x = x + jnp.where(pos < seg - sh, pltpu.roll(x, n - sh, 0), 0.0)
        else:
            x = x + jnp.where(pos >= sh, pltpu.roll(x, sh, 0), 0.0)
        sh *= 2
    return x


def _hgrn_kernel(rev, q_ref, v_ref, f_ref, lb_ref, *rest):
    if rev:
        of_ref, gate_ref, nw_ref, o_ref, state_ref, b_ref, k_ref, ob_ref = rest
    else:
        o_ref, state_ref, b_ref, k_ref = rest
        ob_ref = o_ref.at[0]
    t = q_ref.shape[1]
    nsub = t // SUB
    npair = D_WIDTH // LANES

    @pl.when(pl.program_id(1) == 0)
    def _():
        state_ref[...] = jnp.zeros_like(state_ref)

    fl = f_ref[0]
    log_lb, log_1m_lb, one_m_lb = lb_ref[0:1, :], lb_ref[1:2, :], lb_ref[2:3, :]
    y = log_1m_lb + (jnp.minimum(fl, 0.0) - jnp.log1p(jnp.exp(-jnp.abs(fl))))
    mx = jnp.maximum(log_lb, y)
    logf = mx + jnp.log(jnp.exp(log_lb - mx) + jnp.exp(y - mx))
    b_ref[...] = _segment_cumsum(logf, SUB, rev)
    k_ref[...] = one_m_lb * _sigmoid(-fl)

    bd_f = _head_blockdiag(F32)
    bd = _bf(bd_f)
    row = lax.broadcasted_iota(jnp.int32, (SUB, 1), 0)

    def sub_block(it, carry):
        r0 = pl.multiple_of((nsub - 1 - it if rev else it) * SUB, SUB)
        for p in range(npair):
            cols = slice(p * LANES, (p + 1) * LANES)
            b = b_ref[pl.ds(r0, SUB), cols]
            q = q_ref[0, pl.ds(r0, SUB), cols]
            k = k_ref[pl.ds(r0, SUB), cols]
            v = v_ref[0, pl.ds(r0, SUB), cols]
            b_end = b[0:1, :] if rev else b[SUB - 1:SUB, :]
            state_t = state_ref[p]
            o = _dot_nt(_bf(q * jnp.exp(b)), _bf(state_t))
            xs = []
            for j in range(SUB):
                mask = (row <= j) if rev else (row >= j)
                e = jnp.exp(jnp.where(mask, b - b[j:j + 1, :], NEG))
                xs.append(_bf(q * e * k[j:j + 1, :]))
            ys = _dot(jnp.concatenate(xs, axis=0), bd)
            for j in range(SUB):
                o = o + ys[j * SUB:(j + 1) * SUB] * v[j:j + 1, :]
            ob_ref[pl.ds(r0, SUB), cols] = o
            upd = _dot_tn(_bf(v), _bf(k * jnp.exp(b_end - b)))
            state_ref[p] = state_t * jnp.exp(b_end) + upd * bd_f
        return carry

    lax.fori_loop(0, nsub, sub_block, 0)
    if rev:
        outs = _headnorm_gate(of_ref[0] + ob_ref[...], nw_ref, gate_ref[0])
        for s, val in enumerate(outs):
            o_ref[0, :, s * LANES:(s + 1) * LANES] = val.astype(o_ref.dtype)


def _hgrn_scan(q, v, f, lbp, rev, extra=(), t=128):
    b, s, width = q.shape
    n = s // t
    d = 1 if rev else 0
    idx = (lambda bi, i: (bi, n - 1 - i, 0)) if rev else (lambda bi, i: (bi, i, 0))
    fidx = (lambda bi, i: (bi, n - 1 - i, 1)) if rev else (lambda bi, i: (bi, i, 0))
    in_specs = [pl.BlockSpec((1, t, width), idx), pl.BlockSpec((1, t, width), idx), pl.BlockSpec((1, t, width), fidx),
                pl.BlockSpec((HALO_ROWS, width), lambda bi, i: (0, d))]
    scratch = [pltpu.VMEM((width // LANES, LANES, LANES), F32), pltpu.VMEM((t, width), F32), pltpu.VMEM((t, width), F32)]
    if rev:
        in_specs += [pl.BlockSpec((1, t, width), idx), pl.BlockSpec((1, t, width), idx),
                     pl.BlockSpec((1, LANES), lambda bi, i: (0, 0))]
        scratch.append(pltpu.VMEM((t, width), F32))
    return pl.pallas_call(
        functools.partial(_hgrn_kernel, rev),
        grid=(b, n),
        in_specs=in_specs,
        out_specs=pl.BlockSpec((1, t, width), idx),
        out_shape=jax.ShapeDtypeStruct((b, s, width), BF16 if rev else F32),
        scratch_shapes=scratch,
        compiler_params=_cparams("parallel", "arbitrary"),
        name="hgrn_bwd" if rev else "hgrn_fwd",
    )(q, v, f, lbp, *extra)


def _hgrn_mixer(q, v, f, gate, lb, norm_w):
    lbp = jnp.zeros((HALO_ROWS, 2 * D_WIDTH), F32).at[0].set(jnp.log(lb)).at[1].set(jnp.log1p(-lb)).at[2].set(1.0 - lb)
    o_f = _hgrn_scan(q, v, f, lbp, False)
    return _hgrn_scan(q, v, f, lbp, True, (o_f, gate, jnp.tile(norm_w, 2)[None]))


def _hgrn_lower_bound(d_lb, layer):
    lb_cum = jnp.cumsum(jax.nn.softmax(d_lb.astype(F32), axis=0), axis=0)
    return (lb_cum[layer] - lb_cum[0]).reshape(-1)


def _out_proj_kernel(x_ref, oa_ref, ob0_ref, l0_ref, ob1_ref, l1_ref, ob2_ref, l2_ref, oc_ref, od_ref,
                     wa_ref, wb_ref, wc_ref, wd_ref, y_ref):
    lses = [l0_ref[...], l1_ref[...], l2_ref[...]]
    m = jnp.maximum(jnp.maximum(lses[0], lses[1]), lses[2])
    ws = [jnp.exp(l - m) for l in lses]
    ob = (ws[0] * ob0_ref[...] + ws[1] * ob1_ref[...] + ws[2] * ob2_ref[...]) / (ws[0] + ws[1] + ws[2])
    y = x_ref[...] + _dot(oa_ref[...], wa_ref[...]) + _dot(_bf(ob), wb_ref[...])
    y_ref[...] = y + _dot(oc_ref[...], wc_ref[...]) + _dot(od_ref[...], wd_ref[...])


def _out_proj(x2, oa, ob_lse, oc, od, ws, tm=512):
    n = x2.shape[0]
    acts = [x2, oa] + [a for pair in ob_lse for a in pair] + [oc, od]
    return pl.pallas_call(
        _out_proj_kernel,
        grid=(n // tm,),
        in_specs=[_row_spec(tm, a.shape[1]) for a in acts] + [_const_spec(w.shape) for w in ws],
        out_specs=_row_spec(tm, D_MODEL),
        out_shape=jax.ShapeDtypeStruct((n, D_MODEL), F32),
        compiler_params=_cparams("parallel"),
        name="out_proj",
    )(*acts, *ws)


I_PER_STEP = 8
PEER_SQRT_HALF = 0.7071067811865476


def _top_values(x, count):
    vals = []
    for _ in range(count):
        m = jnp.max(x, axis=0, keepdims=True)
        vals.append(m)
        x = jnp.where(x == m, NEG, x)
    return vals


def _peer_kernel(final, x_ref, nw_ref, wq_ref, keys_ref, u_ref, vt_ref, fnw_ref, y_ref,
                 h_ref, s1_ref, e1_ref, tau_ref, a_ref, acc_ref):
    j = pl.program_id(1)
    tm = x_ref.shape[0]

    @pl.when(j == 0)
    def _():
        h = _bf(_rmsnorm_rows(x_ref[...], nw_ref[...]))
        h_ref[...] = h
        q = _dot(h, wq_ref[...])
        for hd in range(PEER_HEADS):
            s_t = []
            for p in range(2):
                c0 = (2 * hd + p) * PEER_HALF
                s_t.append(_dot_nt(keys_ref[hd, p], _bf(q[:, c0:c0 + PEER_HALF])))
            top = [_top_values(s, PEER_TOPK + 1) for s in s_t]
            cand = jnp.concatenate([top[0][a] + jnp.concatenate(top[1][:PEER_TOPK], axis=0)
                                    for a in range(PEER_TOPK)], axis=0)
            best = _top_values(cand, PEER_TOPK + 1)
            c_max = best[0]
            z = sum(jnp.exp(b - c_max) for b in best[:PEER_TOPK])
            runner_up = jnp.maximum(best[PEER_TOPK], jnp.maximum(top[0][PEER_TOPK] + top[1][0], top[0][0] + top[1][PEER_TOPK]))
            thr = 0.5 * (best[PEER_TOPK - 1] + runner_up)
            s1_ref[hd] = s_t[1]
            e1_ref[hd] = jnp.exp(s_t[1] - top[1][0])
            tau_ref[hd] = thr - s_t[0]
            a_ref[hd] = jnp.exp(s_t[0] - top[0][0]) / z
        acc_ref[...] = jnp.zeros_like(acc_ref)

    act = _dot_nt(u_ref[...], h_ref[...])
    act = 0.5 * act * (1.0 + lax.erf(act * PEER_SQRT_HALF))
    gates = []
    for ii in range(I_PER_STEP):
        i = j * I_PER_STEP + ii
        g = jnp.zeros((N_KEYS, tm), F32)
        for hd in range(PEER_HEADS):
            sel = jnp.where(s1_ref[hd] >= tau_ref[hd, pl.ds(i, 1), :], e1_ref[hd], 0.0)
            g = g + sel * a_ref[hd, pl.ds(i, 1), :]
        gates.append(g)
    w_t = _bf(jnp.concatenate(gates, axis=0) * act)
    acc_ref[...] += _dot(vt_ref[...], w_t)

    @pl.when(j == pl.num_programs(1) - 1)
    def _():
        y = x_ref[...] + acc_ref[...].T
        if final:
            y = _rmsnorm_rows(y, fnw_ref[...])
        y_ref[...] = y


def _peer(x2, nw, wq, keys, u, vt, fnw, final, tm=256):
    n = x2.shape[0]
    ec = I_PER_STEP * N_KEYS
    tok = lambda i, j: (i, 0)
    const2 = lambda i, j: (0, 0)
    stat = pltpu.VMEM((PEER_HEADS, N_KEYS, tm), F32)
    return pl.pallas_call(
        functools.partial(_peer_kernel, final),
        grid=(n // tm, N_EXPERTS // ec),
        in_specs=[pl.BlockSpec((tm, D_MODEL), tok), pl.BlockSpec((1, D_MODEL), const2),
                  pl.BlockSpec(wq.shape, const2), pl.BlockSpec(keys.shape, lambda i, j: (0, 0, 0, 0)),
                  pl.BlockSpec((ec, D_MODEL), lambda i, j: (j, 0)), pl.BlockSpec((D_MODEL, ec), lambda i, j: (0, j)),
                  pl.BlockSpec((1, D_MODEL), const2)],
        out_specs=pl.BlockSpec((tm, D_MODEL), tok),
        out_shape=jax.ShapeDtypeStruct((n, D_MODEL), F32),
        scratch_shapes=[pltpu.VMEM((tm, D_MODEL), BF16), stat, stat, stat, stat, pltpu.VMEM((D_MODEL, tm), F32)],
        compiler_params=_cparams("parallel", "arbitrary"),
        name="peer",
    )(x2, nw, wq, keys, u, vt, fnw)


def _split_cols(w):
    parts, off = [], 0
    for width in IN_SPLITS:
        parts.append(w[:, off:off + width])
        off += width
    return parts


def _axial_tables(seq):
    t = jnp.arange(seq)
    half = HEAD_DIM // 4
    inv = AXIAL_THETA ** (-jnp.arange(half, dtype=F32) / half)
    ang_r = (t // GRID_W).astype(F32)[:, None] * inv[None, :]
    ang_c = (t % GRID_W).astype(F32)[:, None] * inv[None, :]
    zero = jnp.zeros_like(ang_r)
    cos = jnp.concatenate([jnp.cos(ang_r), jnp.cos(ang_r), jnp.cos(ang_c), jnp.cos(ang_c)], axis=1)
    sneg = jnp.concatenate([-jnp.sin(ang_r), zero, -jnp.sin(ang_c), zero], axis=1)
    spos = jnp.concatenate([zero, jnp.sin(ang_r), zero, jnp.sin(ang_c)], axis=1)
    return tuple(jnp.tile(a, (1, 2)) for a in (cos, sneg, spos))


def _rope_tables(seq):
    half = ROT_DIMS // 2
    inv = ROPE_THETA ** (-jnp.arange(half, dtype=F32) / half)
    ang = jnp.arange(seq).astype(F32)[:, None] * inv[None, :]
    rest = HEAD_DIM - ROT_DIMS
    cos = jnp.concatenate([jnp.cos(ang), jnp.cos(ang), jnp.ones((seq, rest), F32)], axis=1)
    sneg = jnp.concatenate([-jnp.sin(ang), jnp.zeros((seq, half + rest), F32)], axis=1)
    spos = jnp.concatenate([jnp.zeros((seq, half), F32), jnp.sin(ang), jnp.zeros((seq, rest), F32)], axis=1)
    return tuple(jnp.tile(a, (1, 2)) for a in (cos, sneg, spos))


def _layout_w_in(w):
    (a_q, a_k, a_v, b_q, b_k, b_v, c_qkv, c_beta, c_a, c_gate, d_q, d_i, d_f, d_gate) = _split_cols(w)
    zero = jnp.zeros((D_MODEL, HEAD_DIM), w.dtype)
    q_slabs = []
    for h in range(A_Q_HEADS):
        qh = a_q[:, h * HEAD_DIM:(h + 1) * HEAD_DIM]
        q_slabs += [qh, zero] if h // A_GROUP == 0 else [zero, qh]
    wa = jnp.concatenate(q_slabs + [a_k, a_v], axis=1)
    b_cols = []
    for g in range(len(DIL_CONFIGS)):
        b_cols += [t[:, g * LANES:(g + 1) * LANES] for t in (b_q, b_k, b_v)]
    wb = jnp.concatenate(b_cols, axis=1)
    pad = jnp.zeros((D_MODEL, LANES - 4 * C_HEADS), w.dtype)
    wc = jnp.concatenate([c_qkv, c_gate, c_beta, c_a, pad], axis=1)
    wd = jnp.concatenate([d_q, d_i, d_f, d_gate], axis=1)
    return tuple(_bf(t) for t in (wa, wb, wc, wd))


def _layout_w_out(w):
    bounds = (0, A_WIDTH, A_WIDTH + LANES, A_WIDTH + LANES + C_WIDTH, A_WIDTH + LANES + C_WIDTH + D_WIDTH)
    return tuple(_bf(w[lo:hi]) for lo, hi in zip(bounds[:-1], bounds[1:]))


def _layer_params(l, norm1_w, w_in, a_qnorm_w, a_knorm_w, c_conv_w, c_a_log, c_dt_bias, c_norm_w, d_lb, d_norm_w, w_out,
                  norm2_w, peer_w_query, peer_sub_keys, peer_u, peer_v):
    return dict(
        norm1=norm1_w[l][None], w_in=_layout_w_in(w_in[l]),
        qnw=jnp.tile(a_qnorm_w[l], 2)[None], knw=jnp.tile(a_knorm_w[l], 2)[None],
        conv_w=c_conv_w[l], a_log=c_a_log[l], dt_bias=c_dt_bias[l], c_norm=c_norm_w[l],
        lb=_hgrn_lower_bound(d_lb, l), d_norm=d_norm_w[l], w_out=_layout_w_out(w_out[l]),
        norm2=norm2_w[l][None], wq=_bf(peer_w_query[l]), keys=_bf(peer_sub_keys[l]),
        u=_bf(peer_u[l]), vt=_bf(peer_v[l].T))


def _trunk(x, layers, final_norm_w):
    b, s, _ = x.shape
    n = b * s
    axial, rope = _axial_tables(s), _rope_tables(s)
    x2 = x.reshape(n, D_MODEL)
    seq = lambda t: t.reshape(b, s, t.shape[-1])
    flat = lambda t: t.reshape(n, t.shape[-1])
    for l, p in enumerate(layers):
        wa, wb, wc, wd = p["w_in"]
        qa, ka, va = _proj_a(x2, p["norm1"], wa, p["qnw"], p["knw"], axial, s)
        zb = _proj_b(x2, p["norm1"], wb, rope, s)
        cqkv, cgate, cba = _proj_plain(x2, p["norm1"], wc, (3 * C_WIDTH, C_WIDTH, LANES), (F32, F32, F32), "proj_c")
        dq, di, df, dgate = _proj_plain(x2, p["norm1"], wd, (D_WIDTH, D_WIDTH, 2 * D_WIDTH, D_WIDTH), (F32,) * 4, "proj_d")
        o_a = _attn_a(seq(qa), seq(ka), seq(va))
        ob_lse = [tuple(flat(t) for t in _attn_b(seq(zb), g)) for g in range(len(DIL_CONFIGS))]
        o_c = _gdn_mixer(seq(cqkv), seq(cba), seq(cgate), p["conv_w"], p["a_log"], p["dt_bias"], p["c_norm"])
        o_d = _hgrn_mixer(seq(dq), seq(di), seq(df), seq(dgate), p["lb"], p["d_norm"])
        x2 = _out_proj(x2, flat(o_a), ob_lse, flat(o_c), flat(o_d), p["w_out"])
        x2 = _peer(x2, p["norm2"], p["wq"], p["keys"], p["u"], p["vt"], final_norm_w[None], l == len(layers) - 1)
    return x2.reshape(b, s, D_MODEL)


def kernel(x_prompt, x_sample, norm1_w, w_in, a_qnorm_w, a_knorm_w, c_conv_w, c_a_log, c_dt_bias, c_norm_w, d_lb, d_norm_w, w_out, norm2_w, peer_w_query, peer_sub_keys, peer_u, peer_v, final_norm_w):
    layers = [_layer_params(l, norm1_w, w_in, a_qnorm_w, a_knorm_w, c_conv_w, c_a_log, c_dt_bias, c_norm_w, d_lb, d_norm_w,
                            w_out, norm2_w, peer_w_query, peer_sub_keys, peer_u, peer_v) for l in range(DEPTH)]
    return (_trunk(x_prompt, layers, final_norm_w), _trunk(x_sample, layers, final_norm_w))
```

```python
import functools

import jax
import jax.numpy as jnp
from jax import lax
from jax.experimental import pallas as pl
from jax.experimental.pallas import tpu as pltpu

F32 = jnp.float32
BF16 = jnp.bfloat16

D_MODEL = 1024
DEPTH = 2
HEAD_DIM = 64
LANES = 128
A_Q_HEADS = 6
A_KV_HEADS = 2
A_GROUP = A_Q_HEADS // A_KV_HEADS
AXIAL_THETA = 10000.0
GRID_W = 64
B_HEADS = 6
DIL_CONFIGS = ((128, 1), (512, 4), (2048, 16))
ROPE_THETA = 500000.0
ROT_DIMS = HEAD_DIM // 4
C_HEADS = 6
CONV_K = 5
CHUNK = 64
D_HEADS = 6
N_KEYS = 128
N_EXPERTS = N_KEYS * N_KEYS
PEER_HEADS = 8
PEER_HALF = 128
PEER_TOPK = 16
EPS = 1e-6
NEG = -1e30

A_WIDTH = A_Q_HEADS * HEAD_DIM
A_KV_WIDTH = A_KV_HEADS * HEAD_DIM
B_WIDTH = B_HEADS * HEAD_DIM
C_WIDTH = C_HEADS * HEAD_DIM
D_WIDTH = D_HEADS * HEAD_DIM
IN_SPLITS = (A_WIDTH, A_KV_WIDTH, A_KV_WIDTH, B_WIDTH, B_WIDTH, B_WIDTH, 3 * C_WIDTH, 2 * C_HEADS, 2 * C_HEADS,
             C_WIDTH, D_WIDTH, D_WIDTH, 2 * D_WIDTH, D_WIDTH)

VMEM_LIMIT_BYTES = 48 * 1024 * 1024


def _cparams(*sem):
    return pltpu.CompilerParams(dimension_semantics=sem, vmem_limit_bytes=VMEM_LIMIT_BYTES)


def _bf(x):
    return x.astype(BF16)


def _dot(a, b):
    return jnp.dot(a, b, preferred_element_type=F32)


def _dot_nt(a, b):
    return lax.dot_general(a, b, (((1,), (1,)), ((), ())), preferred_element_type=F32)


def _dot_tn(a, b):
    return lax.dot_general(a, b, (((0,), (0,)), ((), ())), preferred_element_type=F32)


def _split2(x):
    hi = _bf(x)
    lo = _bf(x - hi.astype(F32))
    return hi, lo


def _head_blockdiag(dtype=BF16):
    r = lax.broadcasted_iota(jnp.int32, (LANES, LANES), 0) // HEAD_DIM
    c = lax.broadcasted_iota(jnp.int32, (LANES, LANES), 1) // HEAD_DIM
    return jnp.where(r == c, 1.0, 0.0).astype(dtype)


def _segsum64(x):
    bd = _head_blockdiag()
    hi, lo = _split2(x)
    return _dot(hi, bd) + _dot(lo, bd)


def _rmsnorm_rows(x, w):
    return x * lax.rsqrt(jnp.mean(x * x, axis=-1, keepdims=True) + EPS) * w


def _rope_slab(y, cos, sneg, spos, shift):
    return y * cos + pltpu.roll(y, LANES - shift, 1) * sneg + pltpu.roll(y, shift, 1) * spos


def _sigmoid(x):
    return 1.0 / (1.0 + jnp.exp(-x))


def _silu(x):
    return x * _sigmoid(x)


def _softplus(x):
    return jnp.maximum(x, 0.0) + jnp.log1p(jnp.exp(-jnp.abs(x)))


def _proj_a_kernel(x_ref, nw_ref, w_ref, qnw_ref, knw_ref, cos_ref, sneg_ref, spos_ref, q_ref, k_ref, v_ref):
    h = _rmsnorm_rows(x_ref[...], nw_ref[...])
    z = _dot(_bf(h), w_ref[...])
    cos, sneg, spos = cos_ref[...], sneg_ref[...], spos_ref[...]

    def norm_rope(zs, nw):
        ms = _segsum64(zs * zs) * (1.0 / HEAD_DIM)
        y = zs * lax.rsqrt(ms + EPS) * nw
        return _rope_slab(y, cos, sneg, spos, HEAD_DIM // 4)

    qnw = qnw_ref[...]
    for s in range(A_Q_HEADS):
        q_ref[:, s * LANES:(s + 1) * LANES] = _bf(norm_rope(z[:, s * LANES:(s + 1) * LANES], qnw) * (HEAD_DIM ** -0.5))
    k_ref[...] = _bf(norm_rope(z[:, 6 * LANES:7 * LANES], knw_ref[...]))
    v_ref[...] = _bf(z[:, 7 * LANES:8 * LANES])


def _proj_b_kernel(x_ref, nw_ref, w_ref, cos_ref, sneg_ref, spos_ref, o_ref):
    h = _rmsnorm_rows(x_ref[...], nw_ref[...])
    z = _dot(_bf(h), w_ref[...])
    cos, sneg, spos = cos_ref[...], sneg_ref[...], spos_ref[...]
    for g in range(len(DIL_CONFIGS)):
        c0 = 3 * g * LANES
        q = _rope_slab(z[:, c0:c0 + LANES], cos, sneg, spos, ROT_DIMS // 2) * (HEAD_DIM ** -0.5)
        k = _rope_slab(z[:, c0 + LANES:c0 + 2 * LANES], cos, sneg, spos, ROT_DIMS // 2)
        o_ref[:, c0:c0 + LANES] = _bf(q)
        o_ref[:, c0 + LANES:c0 + 2 * LANES] = _bf(k)
        o_ref[:, c0 + 2 * LANES:c0 + 3 * LANES] = _bf(z[:, c0 + 2 * LANES:c0 + 3 * LANES])


def _proj_plain_kernel(widths, x_ref, nw_ref, w_ref, *o_refs):
    h = _rmsnorm_rows(x_ref[...], nw_ref[...])
    z = _dot(_bf(h), w_ref[...])
    off = 0
    for width, o_ref in zip(widths, o_refs):
        o_ref[...] = z[:, off:off + width].astype(o_ref.dtype)
        off += width


def _row_spec(tm, width):
    return pl.BlockSpec((tm, width), lambda i: (i, 0))


def _const_spec(shape):
    return pl.BlockSpec(shape, lambda i: (0,) * len(shape))


def _proj_a(x2, nw, w, qnw, knw, tabs, seq, tm=256):
    n = x2.shape[0]
    ns = seq // tm
    tab_spec = pl.BlockSpec((tm, LANES), lambda i: (i % ns, 0))
    return pl.pallas_call(
        _proj_a_kernel,
        grid=(n // tm,),
        in_specs=[_row_spec(tm, D_MODEL), _const_spec((1, D_MODEL)), _const_spec(w.shape), _const_spec((1, LANES)),
                  _const_spec((1, LANES)), tab_spec, tab_spec, tab_spec],
        out_specs=[_row_spec(tm, A_Q_HEADS * LANES), _row_spec(tm, LANES), _row_spec(tm, LANES)],
        out_shape=[jax.ShapeDtypeStruct((n, A_Q_HEADS * LANES), BF16), jax.ShapeDtypeStruct((n, LANES), BF16),
                   jax.ShapeDtypeStruct((n, LANES), BF16)],
        compiler_params=_cparams("parallel"),
        name="proj_a",
    )(x2, nw, w, qnw, knw, *tabs)


def _proj_b(x2, nw, w, tabs, seq, tm=256):
    n = x2.shape[0]
    ns = seq // tm
    tab_spec = pl.BlockSpec((tm, LANES), lambda i: (i % ns, 0))
    return pl.pallas_call(
        _proj_b_kernel,
        grid=(n // tm,),
        in_specs=[_row_spec(tm, D_MODEL), _const_spec((1, D_MODEL)), _const_spec(w.shape), tab_spec, tab_spec, tab_spec],
        out_specs=_row_spec(tm, w.shape[1]),
        out_shape=jax.ShapeDtypeStruct((n, w.shape[1]), BF16),
        compiler_params=_cparams("parallel"),
        name="proj_b",
    )(x2, nw, w, *tabs)


def _proj_plain(x2, nw, w, widths, dtypes, name, tm=256):
    n = x2.shape[0]
    return pl.pallas_call(
        functools.partial(_proj_plain_kernel, widths),
        grid=(n // tm,),
        in_specs=[_row_spec(tm, D_MODEL), _const_spec((1, D_MODEL)), _const_spec(w.shape)],
        out_specs=[_row_spec(tm, wd) for wd in widths],
        out_shape=[jax.ShapeDtypeStruct((n, wd), dt) for wd, dt in zip(widths, dtypes)],
        compiler_params=_cparams("parallel"),
        name=name,
    )(x2, nw, w)


def _lane_is_low():
    return lax.broadcasted_iota(jnp.int32, (1, LANES), 1) < HEAD_DIM


def _attn_a_kernel(tk, q_ref, k_ref, v_ref, o_ref):
    tq = q_ref.shape[1]
    nk = k_ref.shape[1] // tk
    rows = A_GROUP * tq
    per_kv = []
    for kv in range(A_KV_HEADS):
        q = jnp.concatenate([q_ref[0, :, (kv * A_GROUP + g) * LANES:(kv * A_GROUP + g + 1) * LANES]
                             for g in range(A_GROUP)], axis=0)

        def body(c, carry, q=q):
            m, l, acc = carry
            start = pl.multiple_of(c * tk, tk)
            s = _dot_nt(q, k_ref[0, pl.ds(start, tk), :])
            m_new = jnp.maximum(m, jnp.max(s, axis=1, keepdims=True))
            alpha = jnp.exp(m - m_new)
            p = jnp.exp(s - m_new)
            l = alpha * l + jnp.sum(p, axis=1, keepdims=True)
            acc = alpha * acc + _dot(_bf(p), v_ref[0, pl.ds(start, tk), :])
            return m_new, l, acc

        init = (jnp.full((rows, 1), NEG, F32), jnp.zeros((rows, 1), F32), jnp.zeros((rows, LANES), F32))
        _, l, acc = lax.fori_loop(0, nk, body, init)
        per_kv.append(acc / l)
    low = _lane_is_low()
    for j in range(A_Q_HEADS // 2):
        halves = []
        for h in (2 * j, 2 * j + 1):
            kv, g = divmod(h, A_GROUP)
            o = per_kv[kv][g * tq:(g + 1) * tq]
            halves.append(o if kv == h % 2 else pltpu.roll(o, HEAD_DIM, 1))
        o_ref[0, :, j * LANES:(j + 1) * LANES] = jnp.where(low, halves[0], halves[1]).astype(o_ref.dtype)


def _attn_a(q, k, v, tq=256, tk=512):
    b, s, _ = q.shape
    return pl.pallas_call(
        functools.partial(_attn_a_kernel, tk),
        grid=(b, s // tq),
        in_specs=[pl.BlockSpec((1, tq, A_Q_HEADS * LANES), lambda bi, i: (bi, i, 0)),
                  pl.BlockSpec((1, s, LANES), lambda bi, i: (bi, 0, 0)),
                  pl.BlockSpec((1, s, LANES), lambda bi, i: (bi, 0, 0))],
        out_specs=pl.BlockSpec((1, tq, A_WIDTH), lambda bi, i: (bi, i, 0)),
        out_shape=jax.ShapeDtypeStruct((b, s, A_WIDTH), BF16),
        compiler_params=_cparams("parallel", "parallel"),
        name="attn_a",
    )(q, k, v)


B_HALF_WINDOW = 64


def _attn_b_kernel(q_ref, k_ref, v_ref, o_ref, lse_ref):
    tq = q_ref.shape[1]
    length = k_ref.shape[1]
    w = B_HALF_WINDOW
    kwid = tq + 2 * w
    l0 = pl.program_id(2) * tq
    ws = pl.multiple_of(jnp.clip(l0 - w, 0, length - kwid), w)
    kw = k_ref[0, pl.ds(ws, kwid), :]
    vw = v_ref[0, pl.ds(ws, kwid), :]
    qpos = l0 + lax.broadcasted_iota(jnp.int32, (tq, 1), 0)
    kpos = ws + lax.broadcasted_iota(jnp.int32, (1, kwid), 1)
    valid = jnp.abs(kpos - qpos) <= w
    q = q_ref[0]
    low = _lane_is_low()
    outs, lses = [], []
    for j in range(2):
        qj = jnp.where(low if j == 0 else jnp.logical_not(low), q, jnp.zeros_like(q))
        s = jnp.where(valid, _dot_nt(qj, kw), NEG)
        m = jnp.max(s, axis=1, keepdims=True)
        p = jnp.exp(s - m)
        z = jnp.sum(p, axis=1, keepdims=True)
        outs.append(_dot(_bf(p), vw) / z)
        lses.append(jnp.broadcast_to(m + jnp.log(z), (tq, LANES)))
    o_ref[0] = jnp.where(low, outs[0], outs[1])
    lse_ref[0] = jnp.where(low, lses[0], lses[1])


def _attn_b(zb, g, tq=128):
    b, s, width = zb.shape
    dil = DIL_CONFIGS[g][1]
    assert DIL_CONFIGS[g][0] // (2 * dil) == B_HALF_WINDOW
    length = s // dil
    nslab = width // LANES
    z = zb.reshape(b, length, dil * width)
    out = jax.ShapeDtypeStruct((b, length, dil * LANES), F32)
    o, lse = pl.pallas_call(
        _attn_b_kernel,
        grid=(b, dil, length // tq),
        in_specs=[pl.BlockSpec((1, tq, LANES), lambda bi, r, i: (bi, i, r * nslab + 3 * g)),
                  pl.BlockSpec((1, length, LANES), lambda bi, r, i: (bi, 0, r * nslab + 3 * g + 1)),
                  pl.BlockSpec((1, length, LANES), lambda bi, r, i: (bi, 0, r * nslab + 3 * g + 2))],
        out_specs=[pl.BlockSpec((1, tq, LANES), lambda bi, r, i: (bi, i, r))] * 2,
        out_shape=[out, out],
        compiler_params=_cparams("parallel", "parallel", "parallel"),
        name=f"attn_b{g}",
    )(z, z, z)
    return o.reshape(b, s, LANES), lse.reshape(b, s, LANES)


HALO_ROWS = 8


def _dot_hp(a, b):
    a_hi, a_lo = _split2(a)
    b_hi, b_lo = _split2(b)
    return _dot(a_hi, b_hi) + (_dot(a_hi, b_lo) + _dot(a_lo, b_hi))


def _gdn_prep_kernel(x_ref, prev_ref, next_ref, cw_ref, ba_ref, par_ref, qkv_ref, bg_ref, ext_ref):
    t = x_ref.shape[1]
    i = pl.program_id(1)
    last = pl.num_programs(1) - 1
    ext_ref[0:HALO_ROWS] = jnp.where(i > 0, prev_ref[0], 0.0)
    ext_ref[HALO_ROWS:HALO_ROWS + t] = x_ref[0]
    ext_ref[HALO_ROWS + t:] = jnp.where(i < last, next_ref[0], 0.0)
    acc = None
    for k in range(CONV_K):
        term = ext_ref[pl.ds(HALO_ROWS - CONV_K // 2 + k, t), :] * cw_ref[k:k + 1, :]
        acc = term if acc is None else acc + term
    y = _silu(acc)
    for s in range(3 * C_WIDTH // LANES):
        ys = y[:, s * LANES:(s + 1) * LANES]
        if s < 2 * C_WIDTH // LANES:
            ys = ys * lax.rsqrt(_segsum64(ys * ys) + EPS)
            if s < C_WIDTH // LANES:
                ys = ys * (HEAD_DIM ** -0.5)
        qkv_ref[0, :, s * LANES:(s + 1) * LANES] = ys
    z = ba_ref[0]
    lane = lax.broadcasted_iota(jnp.int32, (1, LANES), 1)
    a_log, dt_bias = par_ref[0:1, :], par_ref[1:2, :]
    g = -jnp.exp(a_log) * _softplus(z + dt_bias)
    bg_ref[0] = jnp.where(lane < 2 * C_HEADS, _sigmoid(z), g)


def _gdn_prep(cqkv, cba, conv_w, par, t=256):
    b, s, width = cqkv.shape
    nh = t // HALO_ROWS
    last_halo = s // HALO_ROWS - 1
    return pl.pallas_call(
        _gdn_prep_kernel,
        grid=(b, s // t),
        in_specs=[pl.BlockSpec((1, t, width), lambda bi, i: (bi, i, 0)),
                  pl.BlockSpec((1, HALO_ROWS, width), lambda bi, i: (bi, jnp.maximum(i * nh - 1, 0), 0)),
                  pl.BlockSpec((1, HALO_ROWS, width), lambda bi, i: (bi, jnp.minimum((i + 1) * nh, last_halo), 0)),
                  pl.BlockSpec(conv_w.shape, lambda bi, i: (0, 0)),
                  pl.BlockSpec((1, t, LANES), lambda bi, i: (bi, i, 0)),
                  pl.BlockSpec(par.shape, lambda bi, i: (0, 0))],
        out_specs=[pl.BlockSpec((1, t, width), lambda bi, i: (bi, i, 0)),
                   pl.BlockSpec((1, t, LANES), lambda bi, i: (bi, i, 0))],
        out_shape=[jax.ShapeDtypeStruct((b, s, width), F32), jax.ShapeDtypeStruct((b, s, LANES), F32)],
        scratch_shapes=[pltpu.VMEM((t + 2 * HALO_ROWS, width), F32)],
        compiler_params=_cparams("parallel", "parallel"),
        name="gdn_prep",
    )(cqkv, cqkv, cqkv, conv_w, cba, par)


def _chunk_cumsum(x, rev):
    n = x.shape[0]
    row = lax.broadcasted_iota(jnp.int32, (n, 1), 0)
    sh = 1
    while sh < n:
        if rev:
            x = x + jnp.where(row < n - sh, pltpu.roll(x, n - sh, 0), 0.0)
        else:
            x = x + jnp.where(row >= sh, pltpu.roll(x, sh, 0), 0.0)
        sh *= 2
    return x


def _headnorm_gate(o, nw_ref, gate):
    outs = []
    for s in range(o.shape[1] // LANES):
        os_ = o[:, s * LANES:(s + 1) * LANES]
        ms = _segsum64(os_ * os_) * (1.0 / HEAD_DIM)
        outs.append(os_ * lax.rsqrt(ms + EPS) * nw_ref[...] * _silu(gate[:, s * LANES:(s + 1) * LANES]))
    return outs


def _gdn_kernel(rev, qkv_ref, bg_ref, *rest):
    if rev:
        of_ref, gate_ref, nw_ref, o_ref, state_ref, ob_ref = rest
    else:
        o_ref, state_ref = rest
        ob_ref = o_ref.at[0]
    t = qkv_ref.shape[1]

    @pl.when(pl.program_id(1) == 0)
    def _():
        state_ref[...] = jnp.zeros_like(state_ref)

    ri = lax.broadcasted_iota(jnp.int32, (CHUNK, CHUNK), 0)
    ci = lax.broadcasted_iota(jnp.int32, (CHUNK, CHUNK), 1)
    incl = (ci >= ri) if rev else (ci <= ri)
    strict = (ci > ri) if rev else (ci < ri)
    chunks = list(range(t // CHUNK))
    if rev:
        chunks.reverse()
    units = {}
    for c in chunks:
        r0 = c * CHUNK
        bg = bg_ref[0, r0:r0 + CHUNK, :]
        gc = _chunk_cumsum(bg, rev)
        gc_t = gc.T
        e_gc = jnp.exp(gc)
        g_last = gc[0:1, :] if rev else gc[CHUNK - 1:CHUNK, :]
        e_last = jnp.exp(g_last)
        e_rel = jnp.exp(g_last - gc)
        for h in range(C_HEADS):
            hb = h + (C_HEADS if rev else 0)
            hg = 2 * C_HEADS + hb
            q = qkv_ref[0, r0:r0 + CHUNK, h * HEAD_DIM:(h + 1) * HEAD_DIM]
            k = qkv_ref[0, r0:r0 + CHUNK, C_WIDTH + h * HEAD_DIM:C_WIDTH + (h + 1) * HEAD_DIM]
            v = qkv_ref[0, r0:r0 + CHUNK, 2 * C_WIDTH + h * HEAD_DIM:2 * C_WIDTH + (h + 1) * HEAD_DIM]
            beta = bg[:, hb:hb + 1]
            decay = jnp.exp(jnp.where(incl, gc[:, hg:hg + 1] - gc_t[hg:hg + 1, :], NEG))
            kb = k * beta
            kq = _dot_nt(_bf(jnp.concatenate([kb, q], axis=0)), _bf(k))
            units[c, h] = dict(
                m=jnp.where(strict, kq[:CHUNK] * decay, 0.0), qk=_bf(kq[CHUNK:] * decay),
                r=jnp.concatenate([v * beta, kb * e_gc[:, hg:hg + 1]], axis=1),
                q_in=_bf(q * e_gc[:, hg:hg + 1]), k_out=_bf(k * e_rel[:, hg:hg + 1]), e_last=e_last[:, hg:hg + 1])
    for un in units.values():
        un["p"] = _bf(un["m"])
        un["r"] = un["r"] - _dot(un["p"], _bf(un["r"]))
    for _ in range(5):
        for un in units.values():
            un["p"] = _bf(_dot(un["p"], un["p"]))
        for un in units.values():
            un["r"] = un["r"] + _dot(un["p"], _bf(un["r"]))
    for c in chunks:
        r0 = c * CHUNK
        states = [state_ref[h] for h in range(C_HEADS)]
        v_new = [units[c, h]["r"][:, :HEAD_DIM] - _dot(_bf(units[c, h]["r"][:, HEAD_DIM:]), _bf(states[h]))
                 for h in range(C_HEADS)]
        for h in range(C_HEADS):
            un = units[c, h]
            o = _dot(un["q_in"], _bf(states[h])) + _dot(un["qk"], _bf(v_new[h]))
            ob_ref[r0:r0 + CHUNK, h * HEAD_DIM:(h + 1) * HEAD_DIM] = o
            state_ref[h] = states[h] * un["e_last"] + _dot_tn(un["k_out"], _bf(v_new[h]))
    if rev:
        outs = _headnorm_gate(of_ref[0] + ob_ref[...], nw_ref, gate_ref[0])
        for s, val in enumerate(outs):
            o_ref[0, :, s * LANES:(s + 1) * LANES] = val.astype(o_ref.dtype)


def _gdn_scan(qkv, bg, rev, extra=(), t=256):
    b, s, width = qkv.shape
    n = s // t
    idx = (lambda bi, i: (bi, n - 1 - i, 0)) if rev else (lambda bi, i: (bi, i, 0))
    in_specs = [pl.BlockSpec((1, t, width), idx), pl.BlockSpec((1, t, LANES), idx)]
    scratch = [pltpu.VMEM((C_HEADS, HEAD_DIM, HEAD_DIM), F32)]
    if rev:
        in_specs += [pl.BlockSpec((1, t, C_WIDTH), idx), pl.BlockSpec((1, t, C_WIDTH), idx),
                     pl.BlockSpec((1, LANES), lambda bi, i: (0, 0))]
        scratch.append(pltpu.VMEM((t, C_WIDTH), F32))
    return pl.pallas_call(
        functools.partial(_gdn_kernel, rev),
        grid=(b, n),
        in_specs=in_specs,
        out_specs=pl.BlockSpec((1, t, C_WIDTH), idx),
        out_shape=jax.ShapeDtypeStruct((b, s, C_WIDTH), BF16 if rev else F32),
        scratch_shapes=scratch,
        compiler_params=_cparams("parallel", "arbitrary"),
        name="gdn_bwd" if rev else "gdn_fwd",
    )(qkv, bg, *extra)


def _gdn_mixer(cqkv, cba, gate, conv_w, a_log, dt_bias, norm_w):
    cw = jnp.zeros((HALO_ROWS, cqkv.shape[-1]), F32).at[:CONV_K].set(conv_w)
    par = jnp.zeros((HALO_ROWS, LANES), F32)
    par = par.at[0, 2 * C_HEADS:4 * C_HEADS].set(a_log.reshape(-1)).at[1, 2 * C_HEADS:4 * C_HEADS].set(dt_bias.reshape(-1))
    qkv, bg = _gdn_prep(cqkv, cba, cw, par)
    o_f = _gdn_scan(qkv, bg, False)
    return _gdn_scan(qkv, bg, True, (o_f, gate, jnp.tile(norm_w, 2)[None]))


SUB = 16


def _segment_cumsum(x, seg, rev):
    n = x.shape[0]
    pos = lax.broadcasted_iota(jnp.int32, (n, 1), 0) % seg
    sh = 1
    while sh < seg:
        if rev:
            x = x + jnp.where(pos < seg - sh, pltpu.roll(x, n - sh, 0), 0.0)
        else:
            x = x + jnp.where(pos >= sh, pltpu.roll(x, sh, 0), 0.0)
        sh *= 2
    return x


def _hgrn_kernel(rev, q_ref, v_ref, f_ref, lb_ref, *rest):
    if rev:
        of_ref, gate_ref, nw_ref, o_ref, state_ref, b_ref, k_ref, ob_ref = rest
    else:
        o_ref, state_ref, b_ref, k_ref = rest
        ob_ref = o_ref.at[0]
    t = q_ref.shape[1]
    nsub = t // SUB
    npair = D_WIDTH // LANES

    @pl.when(pl.program_id(1) == 0)
    def _():
        state_ref[...] = jnp.zeros_like(state_ref)

    fl = f_ref[0]
    log_lb, log_1m_lb, one_m_lb = lb_ref[0:1, :], lb_ref[1:2, :], lb_ref[2:3, :]
    y = log_1m_lb + (jnp.minimum(fl, 0.0) - jnp.log1p(jnp.exp(-jnp.abs(fl))))
    mx = jnp.maximum(log_lb, y)
    logf = mx + jnp.log(jnp.exp(log_lb - mx) + jnp.exp(y - mx))
    b_ref[...] = _segment_cumsum(logf, SUB, rev)
    k_ref[...] = one_m_lb * _sigmoid(-fl)

    bd_f = _head_blockdiag(F32)
    bd = _bf(bd_f)
    row = lax.broadcasted_iota(jnp.int32, (SUB, 1), 0)

    def sub_block(it, carry):
        r0 = pl.multiple_of((nsub - 1 - it if rev else it) * SUB, SUB)
        for p in range(npair):
            cols = slice(p * LANES, (p + 1) * LANES)
            b = b_ref[pl.ds(r0, SUB), cols]
            q = q_ref[0, pl.ds(r0, SUB), cols]
            k = k_ref[pl.ds(r0, SUB), cols]
            v = v_ref[0, pl.ds(r0, SUB), cols]
            b_end = b[0:1, :] if rev else b[SUB - 1:SUB, :]
            state_t = state_ref[p]
            o = _dot_nt(_bf(q * jnp.exp(b)), _bf(state_t))
            xs = []
            for j in range(SUB):
                mask = (row <= j) if rev else (row >= j)
                e = jnp.exp(jnp.where(mask, b - b[j:j + 1, :], NEG))
                xs.append(_bf(q * e * k[j:j + 1, :]))
            ys = _dot(jnp.concatenate(xs, axis=0), bd)
            for j in range(SUB):
                o = o + ys[j * SUB:(j + 1) * SUB] * v[j:j + 1, :]
            ob_ref[pl.ds(r0, SUB), cols] = o
            upd = _dot_tn(_bf(v), _bf(k * jnp.exp(b_end - b)))
            state_ref[p] = state_t * jnp.exp(b_end) + upd * bd_f
        return carry

    lax.fori_loop(0, nsub, sub_block, 0)
    if rev:
        outs = _headnorm_gate(of_ref[0] + ob_ref[...], nw_ref, gate_ref[0])
        for s, val in enumerate(outs):
            o_ref[0, :, s * LANES:(s + 1) * LANES] = val.astype(o_ref.dtype)


def _hgrn_scan(q, v, f, lbp, rev, extra=(), t=128):
    b, s, width = q.shape
    n = s // t
    d = 1 if rev else 0
    idx = (lambda bi, i: (bi, n - 1 - i, 0)) if rev else (lambda bi, i: (bi, i, 0))
    fidx = (lambda bi, i: (bi, n - 1 - i, 1)) if rev else (lambda bi, i: (bi, i, 0))
    in_specs = [pl.BlockSpec((1, t, width), idx), pl.BlockSpec((1, t, width), idx), pl.BlockSpec((1, t, width), fidx),
                pl.BlockSpec((HALO_ROWS, width), lambda bi, i: (0, d))]
    scratch = [pltpu.VMEM((width // LANES, LANES, LANES), F32), pltpu.VMEM((t, width), F32), pltpu.VMEM((t, width), F32)]
    if rev:
        in_specs += [pl.BlockSpec((1, t, width), idx), pl.BlockSpec((1, t, width), idx),
                     pl.BlockSpec((1, LANES), lambda bi, i: (0, 0))]
        scratch.append(pltpu.VMEM((t, width), F32))
    return pl.pallas_call(
        functools.partial(_hgrn_kernel, rev),
        grid=(b, n),
        in_specs=in_specs,
        out_specs=pl.BlockSpec((1, t, width), idx),
        out_shape=jax.ShapeDtypeStruct((b, s, width), BF16 if rev else F32),
        scratch_shapes=scratch,
        compiler_params=_cparams("parallel", "arbitrary"),
        name="hgrn_bwd" if rev else "hgrn_fwd",
    )(q, v, f, lbp, *extra)


def _hgrn_mixer(q, v, f, gate, lb, norm_w):
    lbp = jnp.zeros((HALO_ROWS, 2 * D_WIDTH), F32).at[0].set(jnp.log(lb)).at[1].set(jnp.log1p(-lb)).at[2].set(1.0 - lb)
    o_f = _hgrn_scan(q, v, f, lbp, False)
    return _hgrn_scan(q, v, f, lbp, True, (o_f, gate, jnp.tile(norm_w, 2)[None]))


def _hgrn_lower_bound(d_lb, layer):
    lb_cum = jnp.cumsum(jax.nn.softmax(d_lb.astype(F32), axis=0), axis=0)
    return (lb_cum[layer] - lb_cum[0]).reshape(-1)


def _out_proj_kernel(x_ref, oa_ref, ob0_ref, l0_ref, ob1_ref, l1_ref, ob2_ref, l2_ref, oc_ref, od_ref,
                     wa_ref, wb_ref, wc_ref, wd_ref, y_ref):
    lses = [l0_ref[...], l1_ref[...], l2_ref[...]]
    m = jnp.maximum(jnp.maximum(lses[0], lses[1]), lses[2])
    ws = [jnp.exp(l - m) for l in lses]
    ob = (ws[0] * ob0_ref[...] + ws[1] * ob1_ref[...] + ws[2] * ob2_ref[...]) / (ws[0] + ws[1] + ws[2])
    y = x_ref[...] + _dot(oa_ref[...], wa_ref[...]) + _dot(_bf(ob), wb_ref[...])
    y_ref[...] = y + _dot(oc_ref[...], wc_ref[...]) + _dot(od_ref[...], wd_ref[...])


def _out_proj(x2, oa, ob_lse, oc, od, ws, tm=512):
    n = x2.shape[0]
    acts = [x2, oa] + [a for pair in ob_lse for a in pair] + [oc, od]
    return pl.pallas_call(
        _out_proj_kernel,
        grid=(n // tm,),
        in_specs=[_row_spec(tm, a.shape[1]) for a in acts] + [_const_spec(w.shape) for w in ws],
        out_specs=_row_spec(tm, D_MODEL),
        out_shape=jax.ShapeDtypeStruct((n, D_MODEL), F32),
        compiler_params=_cparams("parallel"),
        name="out_proj",
    )(*acts, *ws)


I_PER_STEP = 8
PEER_SQRT_HALF = 0.7071067811865476


def _top_values(x, count):
    vals = []
    for _ in range(count):
        m = jnp.max(x, axis=0, keepdims=True)
        vals.append(m)
        x = jnp.where(x == m, NEG, x)
    return vals


def _peer_stats(x_ref, nw_ref, wq_ref, keys_ref, h_ref, e1_ref, theta_ref, a_ref):
    h = _bf(_rmsnorm_rows(x_ref[...], nw_ref[...]))
    h_ref[...] = h
    q = _dot(h, wq_ref[...])
    ranks = PEER_TOPK + 1
    for hd in range(PEER_HEADS):
        s_t = []
        for p in range(2):
            c0 = (2 * hd + p) * PEER_HALF
            s_t.append(_dot_nt(keys_ref[hd, p], _bf(q[:, c0:c0 + PEER_HALF])))
        top = [_top_values(s, ranks) for s in s_t]
        top1 = jnp.concatenate(top[1], axis=0)
        cand = jnp.concatenate([top[0][i - 1] + top1[:ranks // i] for i in range(1, ranks + 1)], axis=0)
        best = _top_values(cand, ranks)
        z = sum(jnp.exp(b - best[0]) for b in best[:PEER_TOPK])
        thr = 0.5 * (best[PEER_TOPK - 1] + best[PEER_TOPK])
        e1_ref[hd] = jnp.exp(s_t[1] - top[1][0])
        theta_ref[hd] = jnp.exp((thr - top[1][0]) - s_t[0])
        a_ref[hd] = jnp.exp(s_t[0] - top[0][0]) / z


def _peer_weights(jj, u_ref, h_ref, e1_ref, theta_ref, a_ref, w_ref):
    act = _dot_nt(u_ref[...], h_ref[...])
    for ii in range(I_PER_STEP):
        i = jj * I_PER_STEP + ii
        g = None
        for hd in range(PEER_HEADS):
            e1 = e1_ref[hd]
            term = jnp.where(e1 >= theta_ref[hd, pl.ds(i, 1), :], e1, 0.0) * a_ref[hd, pl.ds(i, 1), :]
            g = term if g is None else g + term
        a_ii = act[ii * N_KEYS:(ii + 1) * N_KEYS]
        w_ref[ii * N_KEYS:(ii + 1) * N_KEYS, :] = _bf(g * (0.5 * a_ii * (1.0 + lax.erf(a_ii * PEER_SQRT_HALF))))


def _peer_kernel(final, x_ref, nw_ref, wq_ref, keys_ref, u_ref, vt_ref, fnw_ref, y_ref,
                 h_ref, e1_ref, theta_ref, a_ref, w_ref, acc_ref):
    j = pl.program_id(1)
    last = pl.num_programs(1) - 1
    stats = (e1_ref, theta_ref, a_ref)

    @pl.when(j == 0)
    def _():
        _peer_stats(x_ref, nw_ref, wq_ref, keys_ref, h_ref, *stats)
        _peer_weights(0, u_ref, h_ref, *stats, w_ref.at[0])

    @pl.when(j == 1)
    def _():
        acc_ref[...] = _dot(vt_ref[...], w_ref[0])
        _peer_weights(1, u_ref, h_ref, *stats, w_ref.at[1])

    @pl.when((j > 1) & (j < last))
    def _():
        acc_ref[...] += _dot(vt_ref[...], w_ref[(j - 1) % 2])
        _peer_weights(j, u_ref, h_ref, *stats, w_ref.at[j % 2])

    @pl.when(j == last)
    def _():
        y = x_ref[...] + (acc_ref[...] + _dot(vt_ref[...], w_ref[(last - 1) % 2])).T
        if final:
            y = _rmsnorm_rows(y, fnw_ref[...])
        y_ref[...] = y


def _peer(x2, nw, wq, keys, u, vt, fnw, final, tm=512):
    n = x2.shape[0]
    ec = I_PER_STEP * N_KEYS
    nblk = N_EXPERTS // ec
    tok = lambda i, j: (i, 0)
    const2 = lambda i, j: (0, 0)
    stat = pltpu.VMEM((PEER_HEADS, N_KEYS, tm), F32)
    return pl.pallas_call(
        functools.partial(_peer_kernel, final),
        grid=(n // tm, nblk + 1),
        in_specs=[pl.BlockSpec((tm, D_MODEL), tok), pl.BlockSpec((1, D_MODEL), const2),
                  pl.BlockSpec(wq.shape, const2), pl.BlockSpec(keys.shape, lambda i, j: (0, 0, 0, 0)),
                  pl.BlockSpec((ec, D_MODEL), lambda i, j: (jnp.minimum(j, nblk - 1), 0)),
                  pl.BlockSpec((D_MODEL, ec), lambda i, j: (0, jnp.maximum(j - 1, 0))),
                  pl.BlockSpec((1, D_MODEL), const2)],
        out_specs=pl.BlockSpec((tm, D_MODEL), tok),
        out_shape=jax.ShapeDtypeStruct((n, D_MODEL), F32),
        scratch_shapes=[pltpu.VMEM((tm, D_MODEL), BF16), stat, stat, stat, pltpu.VMEM((2, ec, tm), BF16),
                        pltpu.VMEM((D_MODEL, tm), F32)],
        compiler_params=_cparams("parallel", "arbitrary"),
        name="peer",
    )(x2, nw, wq, keys, u, vt, fnw)


def _split_cols(w):
    parts, off = [], 0
    for width in IN_SPLITS:
        parts.append(w[:, off:off + width])
        off += width
    return parts


def _axial_tables(seq):
    t = jnp.arange(seq)
    half = HEAD_DIM // 4
    inv = AXIAL_THETA ** (-jnp.arange(half, dtype=F32) / half)
    ang_r = (t // GRID_W).astype(F32)[:, None] * inv[None, :]
    ang_c = (t % GRID_W).astype(F32)[:, None] * inv[None, :]
    zero = jnp.zeros_like(ang_r)
    cos = jnp.concatenate([jnp.cos(ang_r), jnp.cos(ang_r), jnp.cos(ang_c), jnp.cos(ang_c)], axis=1)
    sneg = jnp.concatenate([-jnp.sin(ang_r), zero, -jnp.sin(ang_c), zero], axis=1)
    spos = jnp.concatenate([zero, jnp.sin(ang_r), zero, jnp.sin(ang_c)], axis=1)
    return tuple(jnp.tile(a, (1, 2)) for a in (cos, sneg, spos))


def _rope_tables(seq):
    half = ROT_DIMS // 2
    inv = ROPE_THETA ** (-jnp.arange(half, dtype=F32) / half)
    ang = jnp.arange(seq).astype(F32)[:, None] * inv[None, :]
    rest = HEAD_DIM - ROT_DIMS
    cos = jnp.concatenate([jnp.cos(ang), jnp.cos(ang), jnp.ones((seq, rest), F32)], axis=1)
    sneg = jnp.concatenate([-jnp.sin(ang), jnp.zeros((seq, half + rest), F32)], axis=1)
    spos = jnp.concatenate([jnp.zeros((seq, half), F32), jnp.sin(ang), jnp.zeros((seq, rest), F32)], axis=1)
    return tuple(jnp.tile(a, (1, 2)) for a in (cos, sneg, spos))


def _layout_w_in(w):
    (a_q, a_k, a_v, b_q, b_k, b_v, c_qkv, c_beta, c_a, c_gate, d_q, d_i, d_f, d_gate) = _split_cols(w)
    zero = jnp.zeros((D_MODEL, HEAD_DIM), w.dtype)
    q_slabs = []
    for h in range(A_Q_HEADS):
        qh = a_q[:, h * HEAD_DIM:(h + 1) * HEAD_DIM]
        q_slabs += [qh, zero] if h // A_GROUP == 0 else [zero, qh]
    wa = jnp.concatenate(q_slabs + [a_k, a_v], axis=1)
    b_cols = []
    for g in range(len(DIL_CONFIGS)):
        b_cols += [t[:, g * LANES:(g + 1) * LANES] for t in (b_q, b_k, b_v)]
    wb = jnp.concatenate(b_cols, axis=1)
    pad = jnp.zeros((D_MODEL, LANES - 4 * C_HEADS), w.dtype)
    wc = jnp.concatenate([c_qkv, c_gate, c_beta, c_a, pad], axis=1)
    wd = jnp.concatenate([d_q, d_i, d_f, d_gate], axis=1)
    return tuple(_bf(t) for t in (wa, wb, wc, wd))


def _layout_w_out(w):
    bounds = (0, A_WIDTH, A_WIDTH + LANES, A_WIDTH + LANES + C_WIDTH, A_WIDTH + LANES + C_WIDTH + D_WIDTH)
    return tuple(_bf(w[lo:hi]) for lo, hi in zip(bounds[:-1], bounds[1:]))


def _layer_params(l, norm1_w, w_in, a_qnorm_w, a_knorm_w, c_conv_w, c_a_log, c_dt_bias, c_norm_w, d_lb, d_norm_w, w_out,
                  norm2_w, peer_w_query, peer_sub_keys, peer_u, peer_v):
    return dict(
        norm1=norm1_w[l][None], w_in=_layout_w_in(w_in[l]),
        qnw=jnp.tile(a_qnorm_w[l], 2)[None], knw=jnp.tile(a_knorm_w[l], 2)[None],
        conv_w=c_conv_w[l], a_log=c_a_log[l], dt_bias=c_dt_bias[l], c_norm=c_norm_w[l],
        lb=_hgrn_lower_bound(d_lb, l), d_norm=d_norm_w[l], w_out=_layout_w_out(w_out[l]),
        norm2=norm2_w[l][None], wq=_bf(peer_w_query[l]), keys=_bf(peer_sub_keys[l]),
        u=_bf(peer_u[l]), vt=_bf(peer_v[l].T))


def _trunk(x, layers, final_norm_w):
    b, s, _ = x.shape
    n = b * s
    axial, rope = _axial_tables(s), _rope_tables(s)
    x2 = x.reshape(n, D_MODEL)
    seq = lambda t: t.reshape(b, s, t.shape[-1])
    flat = lambda t: t.reshape(n, t.shape[-1])
    for l, p in enumerate(layers):
        wa, wb, wc, wd = p["w_in"]
        qa, ka, va = _proj_a(x2, p["norm1"], wa, p["qnw"], p["knw"], axial, s)
        zb = _proj_b(x2, p["norm1"], wb, rope, s)
        cqkv, cgate, cba = _proj_plain(x2, p["norm1"], wc, (3 * C_WIDTH, C_WIDTH, LANES), (F32, F32, F32), "proj_c")
        dq, di, df, dgate = _proj_plain(x2, p["norm1"], wd, (D_WIDTH, D_WIDTH, 2 * D_WIDTH, D_WIDTH), (F32,) * 4, "proj_d")
        o_a = _attn_a(seq(qa), seq(ka), seq(va))
        ob_lse = [tuple(flat(t) for t in _attn_b(seq(zb), g)) for g in range(len(DIL_CONFIGS))]
        o_c = _gdn_mixer(seq(cqkv), seq(cba), seq(cgate), p["conv_w"], p["a_log"], p["dt_bias"], p["c_norm"])
        o_d = _hgrn_mixer(seq(dq), seq(di), seq(df), seq(dgate), p["lb"], p["d_norm"])
        x2 = _out_proj(x2, flat(o_a), ob_lse, flat(o_c), flat(o_d), p["w_out"])
        x2 = _peer(x2, p["norm2"], p["wq"], p["keys"], p["u"], p["vt"], final_norm_w[None], l == len(layers) - 1)
    return x2.reshape(b, s, D_MODEL)


def kernel(x_prompt, x_sample, norm1_w, w_in, a_qnorm_w, a_knorm_w, c_conv_w, c_a_log, c_dt_bias, c_norm_w, d_lb, d_norm_w, w_out, norm2_w, peer_w_query, peer_sub_keys, peer_u, peer_v, final_norm_w):
    layers = [_layer_params(l, norm1_w, w_in, a_qnorm_w, a_knorm_w, c_conv_w, c_a_log, c_dt_bias, c_norm_w, d_lb, d_norm_w,
                            w_out, norm2_w, peer_w_query, peer_sub_keys, peer_u, peer_v) for l in range(DEPTH)]
    return (_trunk(x_prompt, layers, final_norm_w), _trunk(x_sample, layers, final_norm_w))
```

```python
import functools

import jax
import jax.numpy as jnp
from jax import lax
from jax.experimental import pallas as pl
from jax.experimental.pallas import tpu as pltpu

F32 = jnp.float32
BF16 = jnp.bfloat16

D_MODEL = 1024
DEPTH = 2
HEAD_DIM = 64
LANES = 128
A_Q_HEADS = 6
A_KV_HEADS = 2
A_GROUP = A_Q_HEADS // A_KV_HEADS
AXIAL_THETA = 10000.0
GRID_W = 64
B_HEADS = 6
DIL_CONFIGS = ((128, 1), (512, 4), (2048, 16))
ROPE_THETA = 500000.0
ROT_DIMS = HEAD_DIM // 4
C_HEADS = 6
CONV_K = 5
CHUNK = 64
D_HEADS = 6
N_KEYS = 128
N_EXPERTS = N_KEYS * N_KEYS
PEER_HEADS = 8
PEER_HALF = 128
PEER_TOPK = 16
EPS = 1e-6
NEG = -1e30

A_WIDTH = A_Q_HEADS * HEAD_DIM
A_KV_WIDTH = A_KV_HEADS * HEAD_DIM
B_WIDTH = B_HEADS * HEAD_DIM
C_WIDTH = C_HEADS * HEAD_DIM
D_WIDTH = D_HEADS * HEAD_DIM
IN_SPLITS = (A_WIDTH, A_KV_WIDTH, A_KV_WIDTH, B_WIDTH, B_WIDTH, B_WIDTH, 3 * C_WIDTH, 2 * C_HEADS, 2 * C_HEADS,
             C_WIDTH, D_WIDTH, D_WIDTH, 2 * D_WIDTH, D_WIDTH)

VMEM_LIMIT_BYTES = 48 * 1024 * 1024


def _cparams(*sem):
    return pltpu.CompilerParams(dimension_semantics=sem, vmem_limit_bytes=VMEM_LIMIT_BYTES)


def _bf(x):
    return x.astype(BF16)


def _dot(a, b):
    return jnp.dot(a, b, preferred_element_type=F32)


def _dot_nt(a, b):
    return lax.dot_general(a, b, (((1,), (1,)), ((), ())), preferred_element_type=F32)


def _dot_tn(a, b):
    return lax.dot_general(a, b, (((0,), (0,)), ((), ())), preferred_element_type=F32)


def _split2(x):
    hi = _bf(x)
    lo = _bf(x - hi.astype(F32))
    return hi, lo


def _head_blockdiag(dtype=BF16):
    r = lax.broadcasted_iota(jnp.int32, (LANES, LANES), 0) // HEAD_DIM
    c = lax.broadcasted_iota(jnp.int32, (LANES, LANES), 1) // HEAD_DIM
    return jnp.where(r == c, 1.0, 0.0).astype(dtype)


def _segsum64(x):
    bd = _head_blockdiag()
    hi, lo = _split2(x)
    return _dot(hi, bd) + _dot(lo, bd)


def _rmsnorm_rows(x, w):
    return x * lax.rsqrt(jnp.mean(x * x, axis=-1, keepdims=True) + EPS) * w


def _rope_slab(y, cos, sneg, spos, shift):
    return y * cos + pltpu.roll(y, LANES - shift, 1) * sneg + pltpu.roll(y, shift, 1) * spos


def _sigmoid(x):
    return 1.0 / (1.0 + jnp.exp(-x))


def _silu(x):
    return x * _sigmoid(x)


def _softplus(x):
    return jnp.maximum(x, 0.0) + jnp.log1p(jnp.exp(-jnp.abs(x)))


def _proj_a_kernel(x_ref, nw_ref, w_ref, qnw_ref, knw_ref, cos_ref, sneg_ref, spos_ref, q_ref, k_ref, v_ref):
    h = _rmsnorm_rows(x_ref[...], nw_ref[...])
    z = _dot(_bf(h), w_ref[...])
    cos, sneg, spos = cos_ref[...], sneg_ref[...], spos_ref[...]

    def norm_rope(zs, nw):
        ms = _segsum64(zs * zs) * (1.0 / HEAD_DIM)
        y = zs * lax.rsqrt(ms + EPS) * nw
        return _rope_slab(y, cos, sneg, spos, HEAD_DIM // 4)

    qnw = qnw_ref[...]
    for s in range(A_Q_HEADS):
        q_ref[:, s * LANES:(s + 1) * LANES] = _bf(norm_rope(z[:, s * LANES:(s + 1) * LANES], qnw) * (HEAD_DIM ** -0.5))
    k_ref[...] = _bf(norm_rope(z[:, 6 * LANES:7 * LANES], knw_ref[...]))
    v_ref[...] = _bf(z[:, 7 * LANES:8 * LANES])


def _proj_b_kernel(x_ref, nw_ref, w_ref, cos_ref, sneg_ref, spos_ref, o_ref):
    h = _rmsnorm_rows(x_ref[...], nw_ref[...])
    z = _dot(_bf(h), w_ref[...])
    cos, sneg, spos = cos_ref[...], sneg_ref[...], spos_ref[...]
    for g in range(len(DIL_CONFIGS)):
        c0 = 3 * g * LANES
        q = _rope_slab(z[:, c0:c0 + LANES], cos, sneg, spos, ROT_DIMS // 2) * (HEAD_DIM ** -0.5)
        k = _rope_slab(z[:, c0 + LANES:c0 + 2 * LANES], cos, sneg, spos, ROT_DIMS // 2)
        o_ref[:, c0:c0 + LANES] = _bf(q)
        o_ref[:, c0 + LANES:c0 + 2 * LANES] = _bf(k)
        o_ref[:, c0 + 2 * LANES:c0 + 3 * LANES] = _bf(z[:, c0 + 2 * LANES:c0 + 3 * LANES])


def _proj_plain_kernel(widths, x_ref, nw_ref, w_ref, *o_refs):
    h = _rmsnorm_rows(x_ref[...], nw_ref[...])
    z = _dot(_bf(h), w_ref[...])
    off = 0
    for width, o_ref in zip(widths, o_refs):
        o_ref[...] = z[:, off:off + width].astype(o_ref.dtype)
        off += width


def _row_spec(tm, width):
    return pl.BlockSpec((tm, width), lambda i: (i, 0))


def _const_spec(shape):
    return pl.BlockSpec(shape, lambda i: (0,) * len(shape))


def _proj_a(x2, nw, w, qnw, knw, tabs, seq, tm=256):
    n = x2.shape[0]
    ns = seq // tm
    tab_spec = pl.BlockSpec((tm, LANES), lambda i: (i % ns, 0))
    return pl.pallas_call(
        _proj_a_kernel,
        grid=(n // tm,),
        in_specs=[_row_spec(tm, D_MODEL), _const_spec((1, D_MODEL)), _const_spec(w.shape), _const_spec((1, LANES)),
                  _const_spec((1, LANES)), tab_spec, tab_spec, tab_spec],
        out_specs=[_row_spec(tm, A_Q_HEADS * LANES), _row_spec(tm, LANES), _row_spec(tm, LANES)],
        out_shape=[jax.ShapeDtypeStruct((n, A_Q_HEADS * LANES), BF16), jax.ShapeDtypeStruct((n, LANES), BF16),
                   jax.ShapeDtypeStruct((n, LANES), BF16)],
        compiler_params=_cparams("parallel"),
        name="proj_a",
    )(x2, nw, w, qnw, knw, *tabs)


def _proj_b(x2, nw, w, tabs, seq, tm=256):
    n = x2.shape[0]
    ns = seq // tm
    tab_spec = pl.BlockSpec((tm, LANES), lambda i: (i % ns, 0))
    return pl.pallas_call(
        _proj_b_kernel,
        grid=(n // tm,),
        in_specs=[_row_spec(tm, D_MODEL), _const_spec((1, D_MODEL)), _const_spec(w.shape), tab_spec, tab_spec, tab_spec],
        out_specs=_row_spec(tm, w.shape[1]),
        out_shape=jax.ShapeDtypeStruct((n, w.shape[1]), BF16),
        compiler_params=_cparams("parallel"),
        name="proj_b",
    )(x2, nw, w, *tabs)


def _proj_plain(x2, nw, w, widths, dtypes, name, tm=256):
    n = x2.shape[0]
    return pl.pallas_call(
        functools.partial(_proj_plain_kernel, widths),
        grid=(n // tm,),
        in_specs=[_row_spec(tm, D_MODEL), _const_spec((1, D_MODEL)), _const_spec(w.shape)],
        out_specs=[_row_spec(tm, wd) for wd in widths],
        out_shape=[jax.ShapeDtypeStruct((n, wd), dt) for wd, dt in zip(widths, dtypes)],
        compiler_params=_cparams("parallel"),
        name=name,
    )(x2, nw, w)


def _lane_is_low():
    return lax.broadcasted_iota(jnp.int32, (1, LANES), 1) < HEAD_DIM


def _attn_a_kernel(tk, q_ref, k_ref, v_ref, o_ref):
    tq = q_ref.shape[1]
    nk = k_ref.shape[1] // tk
    rows = A_GROUP * tq
    low = _lane_is_low()
    per_kv = []
    for kv in range(A_KV_HEADS):
        q = jnp.concatenate([q_ref[0, :, (kv * A_GROUP + g) * LANES:(kv * A_GROUP + g + 1) * LANES]
                             for g in range(A_GROUP)], axis=0)
        own = low if kv == 0 else jnp.logical_not(low)

        def body(c, carry, q=q, own=own):
            m, acc = carry
            start = pl.multiple_of(c * tk, tk)
            s = _dot_nt(q, k_ref[0, pl.ds(start, tk), :])
            m_new = jnp.maximum(m, jnp.max(s, axis=1, keepdims=True))
            alpha = jnp.exp(m - m_new)
            p = jnp.exp((s - m_new).astype(BF16))
            v = v_ref[0, pl.ds(start, tk), :]
            acc = alpha * acc + _dot(p, jnp.where(own, v, jnp.ones_like(v)))
            return m_new, acc

        init = (jnp.full((rows, 1), NEG, F32), jnp.zeros((rows, LANES), F32))
        _, acc = lax.fori_loop(0, nk, body, init)
        per_kv.append(acc / pltpu.roll(acc, HEAD_DIM, 1))
    for j in range(A_Q_HEADS // 2):
        halves = []
        for h in (2 * j, 2 * j + 1):
            kv, g = divmod(h, A_GROUP)
            o = per_kv[kv][g * tq:(g + 1) * tq]
            halves.append(o if kv == h % 2 else pltpu.roll(o, HEAD_DIM, 1))
        o_ref[0, :, j * LANES:(j + 1) * LANES] = jnp.where(low, halves[0], halves[1]).astype(o_ref.dtype)


def _attn_a(q, k, v, tq=256, tk_max=4096):
    b, s, _ = q.shape
    tk = min(s, tk_max)
    return pl.pallas_call(
        functools.partial(_attn_a_kernel, tk),
        grid=(b, s // tq),
        in_specs=[pl.BlockSpec((1, tq, A_Q_HEADS * LANES), lambda bi, i: (bi, i, 0)),
                  pl.BlockSpec((1, s, LANES), lambda bi, i: (bi, 0, 0)),
                  pl.BlockSpec((1, s, LANES), lambda bi, i: (bi, 0, 0))],
        out_specs=pl.BlockSpec((1, tq, A_WIDTH), lambda bi, i: (bi, i, 0)),
        out_shape=jax.ShapeDtypeStruct((b, s, A_WIDTH), BF16),
        compiler_params=_cparams("parallel", "parallel"),
        name="attn_a",
    )(q, k, v)


B_HALF_WINDOW = 64


def _attn_b_kernel(q_ref, k_ref, v_ref, o_ref, lse_ref):
    tq = q_ref.shape[1]
    length = k_ref.shape[1]
    w = B_HALF_WINDOW
    kwid = tq + 2 * w
    l0 = pl.program_id(2) * tq
    ws = pl.multiple_of(jnp.clip(l0 - w, 0, length - kwid), w)
    kw = k_ref[0, pl.ds(ws, kwid), :]
    vw = v_ref[0, pl.ds(ws, kwid), :]
    qpos = l0 + lax.broadcasted_iota(jnp.int32, (tq, 1), 0)
    kpos = ws + lax.broadcasted_iota(jnp.int32, (1, kwid), 1)
    valid = jnp.abs(kpos - qpos) <= w
    q = q_ref[0]
    low = _lane_is_low()
    owns = [low, jnp.logical_not(low)]
    ss = [jnp.where(valid, _dot_nt(jnp.where(own, q, jnp.zeros_like(q)), kw), NEG) for own in owns]
    ms = [jnp.max(s, axis=1, keepdims=True) for s in ss]
    ps = [jnp.exp((s - m).astype(BF16)) for s, m in zip(ss, ms)]
    pvs = [_dot(p, jnp.where(own, vw, jnp.ones_like(vw))) for p, own in zip(ps, owns)]
    zs = [pltpu.roll(pv, HEAD_DIM, 1) for pv in pvs]
    o_ref[0] = jnp.where(low, pvs[0] / zs[0], pvs[1] / zs[1])
    lse_ref[0] = jnp.where(low, ms[0] + jnp.log(zs[0]), ms[1] + jnp.log(zs[1]))


def _attn_b(zb, g):
    b, s, width = zb.shape
    dil = DIL_CONFIGS[g][1]
    assert DIL_CONFIGS[g][0] // (2 * dil) == B_HALF_WINDOW
    length = s // dil
    tq = 256 if length >= 512 else 128
    nslab = width // LANES
    z = zb.reshape(b, length, dil * width)
    out = jax.ShapeDtypeStruct((b, length, dil * LANES), F32)
    o, lse = pl.pallas_call(
        _attn_b_kernel,
        grid=(b, dil, length // tq),
        in_specs=[pl.BlockSpec((1, tq, LANES), lambda bi, r, i: (bi, i, r * nslab + 3 * g)),
                  pl.BlockSpec((1, length, LANES), lambda bi, r, i: (bi, 0, r * nslab + 3 * g + 1)),
                  pl.BlockSpec((1, length, LANES), lambda bi, r, i: (bi, 0, r * nslab + 3 * g + 2))],
        out_specs=[pl.BlockSpec((1, tq, LANES), lambda bi, r, i: (bi, i, r))] * 2,
        out_shape=[out, out],
        compiler_params=_cparams("parallel", "parallel", "parallel"),
        name=f"attn_b{g}",
    )(z, z, z)
    return o.reshape(b, s, LANES), lse.reshape(b, s, LANES)


HALO_ROWS = 8


def _dot_hp(a, b):
    a_hi, a_lo = _split2(a)
    b_hi, b_lo = _split2(b)
    return _dot(a_hi, b_hi) + (_dot(a_hi, b_lo) + _dot(a_lo, b_hi))


def _gdn_prep_kernel(x_ref, prev_ref, next_ref, cw_ref, ba_ref, par_ref, qkv_ref, bg_ref, ext_ref):
    t = x_ref.shape[1]
    i = pl.program_id(1)
    last = pl.num_programs(1) - 1
    ext_ref[0:HALO_ROWS] = jnp.where(i > 0, prev_ref[0], 0.0)
    ext_ref[HALO_ROWS:HALO_ROWS + t] = x_ref[0]
    ext_ref[HALO_ROWS + t:] = jnp.where(i < last, next_ref[0], 0.0)
    acc = None
    for k in range(CONV_K):
        term = ext_ref[pl.ds(HALO_ROWS - CONV_K // 2 + k, t), :] * cw_ref[k:k + 1, :]
        acc = term if acc is None else acc + term
    y = _silu(acc)
    for s in range(3 * C_WIDTH // LANES):
        ys = y[:, s * LANES:(s + 1) * LANES]
        if s < 2 * C_WIDTH // LANES:
            ys = ys * lax.rsqrt(_segsum64(ys * ys) + EPS)
            if s < C_WIDTH // LANES:
                ys = ys * (HEAD_DIM ** -0.5)
        qkv_ref[0, :, s * LANES:(s + 1) * LANES] = ys
    z = ba_ref[0]
    lane = lax.broadcasted_iota(jnp.int32, (1, LANES), 1)
    a_log, dt_bias = par_ref[0:1, :], par_ref[1:2, :]
    g = -jnp.exp(a_log) * _softplus(z + dt_bias)
    bg_ref[0] = jnp.where(lane < 2 * C_HEADS, _sigmoid(z), g)


def _gdn_prep(cqkv, cba, conv_w, par, t=256):
    b, s, width = cqkv.shape
    nh = t // HALO_ROWS
    last_halo = s // HALO_ROWS - 1
    return pl.pallas_call(
        _gdn_prep_kernel,
        grid=(b, s // t),
        in_specs=[pl.BlockSpec((1, t, width), lambda bi, i: (bi, i, 0)),
                  pl.BlockSpec((1, HALO_ROWS, width), lambda bi, i: (bi, jnp.maximum(i * nh - 1, 0), 0)),
                  pl.BlockSpec((1, HALO_ROWS, width), lambda bi, i: (bi, jnp.minimum((i + 1) * nh, last_halo), 0)),
                  pl.BlockSpec(conv_w.shape, lambda bi, i: (0, 0)),
                  pl.BlockSpec((1, t, LANES), lambda bi, i: (bi, i, 0)),
                  pl.BlockSpec(par.shape, lambda bi, i: (0, 0))],
        out_specs=[pl.BlockSpec((1, t, width), lambda bi, i: (bi, i, 0)),
                   pl.BlockSpec((1, t, LANES), lambda bi, i: (bi, i, 0))],
        out_shape=[jax.ShapeDtypeStruct((b, s, width), F32), jax.ShapeDtypeStruct((b, s, LANES), F32)],
        scratch_shapes=[pltpu.VMEM((t + 2 * HALO_ROWS, width), F32)],
        compiler_params=_cparams("parallel", "parallel"),
        name="gdn_prep",
    )(cqkv, cqkv, cqkv, conv_w, cba, par)


def _chunk_cumsum(x, rev):
    n = x.shape[0]
    row = lax.broadcasted_iota(jnp.int32, (n, 1), 0)
    sh = 1
    while sh < n:
        if rev:
            x = x + jnp.where(row < n - sh, pltpu.roll(x, n - sh, 0), 0.0)
        else:
            x = x + jnp.where(row >= sh, pltpu.roll(x, sh, 0), 0.0)
        sh *= 2
    return x


def _headnorm_gate(o, nw_ref, gate):
    outs = []
    for s in range(o.shape[1] // LANES):
        os_ = o[:, s * LANES:(s + 1) * LANES]
        ms = _segsum64(os_ * os_) * (1.0 / HEAD_DIM)
        outs.append(os_ * lax.rsqrt(ms + EPS) * nw_ref[...] * _silu(gate[:, s * LANES:(s + 1) * LANES]))
    return outs


def _gdn_kernel(rev, qkv_ref, bg_ref, *rest):
    if rev:
        of_ref, gate_ref, nw_ref, o_ref, state_ref, ob_ref = rest
    else:
        o_ref, state_ref = rest
        ob_ref = o_ref.at[0]
    t = qkv_ref.shape[1]

    @pl.when(pl.program_id(1) == 0)
    def _():
        state_ref[...] = jnp.zeros_like(state_ref)

    ri = lax.broadcasted_iota(jnp.int32, (CHUNK, CHUNK), 0)
    ci = lax.broadcasted_iota(jnp.int32, (CHUNK, CHUNK), 1)
    incl = (ci >= ri) if rev else (ci <= ri)
    strict = (ci > ri) if rev else (ci < ri)
    chunks = list(range(t // CHUNK))
    if rev:
        chunks.reverse()
    units = {}
    for c in chunks:
        r0 = c * CHUNK
        bg = bg_ref[0, r0:r0 + CHUNK, :]
        gc = _chunk_cumsum(bg, rev)
        gc_t = gc.T
        e_gc = jnp.exp(gc)
        g_last = gc[0:1, :] if rev else gc[CHUNK - 1:CHUNK, :]
        e_last = jnp.exp(g_last)
        e_rel = jnp.exp(g_last - gc)
        for h in range(C_HEADS):
            hb = h + (C_HEADS if rev else 0)
            hg = 2 * C_HEADS + hb
            q = qkv_ref[0, r0:r0 + CHUNK, h * HEAD_DIM:(h + 1) * HEAD_DIM]
            k = qkv_ref[0, r0:r0 + CHUNK, C_WIDTH + h * HEAD_DIM:C_WIDTH + (h + 1) * HEAD_DIM]
            v = qkv_ref[0, r0:r0 + CHUNK, 2 * C_WIDTH + h * HEAD_DIM:2 * C_WIDTH + (h + 1) * HEAD_DIM]
            beta = bg[:, hb:hb + 1]
            decay = jnp.exp(jnp.where(incl, gc[:, hg:hg + 1] - gc_t[hg:hg + 1, :], NEG))
            kb = k * beta
            kq = _dot_nt(_bf(jnp.concatenate([kb, q], axis=0)), _bf(k))
            units[c, h] = dict(
                m=jnp.where(strict, kq[:CHUNK] * decay, 0.0), qk=_bf(kq[CHUNK:] * decay),
                r=jnp.concatenate([v * beta, kb * e_gc[:, hg:hg + 1]], axis=1),
                q_in=_bf(q * e_gc[:, hg:hg + 1]), k_out=_bf(k * e_rel[:, hg:hg + 1]), e_last=e_last[:, hg:hg + 1])
    for un in units.values():
        un["p"] = _bf(un["m"])
        un["r"] = un["r"] - _dot(un["p"], _bf(un["r"]))
    for _ in range(5):
        for un in units.values():
            un["p"] = _bf(_dot(un["p"], un["p"]))
        for un in units.values():
            un["r"] = un["r"] + _dot(un["p"], _bf(un["r"]))
    for c in chunks:
        r0 = c * CHUNK
        states = [state_ref[h] for h in range(C_HEADS)]
        v_new = [units[c, h]["r"][:, :HEAD_DIM] - _dot(_bf(units[c, h]["r"][:, HEAD_DIM:]), _bf(states[h]))
                 for h in range(C_HEADS)]
        for h in range(C_HEADS):
            un = units[c, h]
            o = _dot(un["q_in"], _bf(states[h])) + _dot(un["qk"], _bf(v_new[h]))
            ob_ref[r0:r0 + CHUNK, h * HEAD_DIM:(h + 1) * HEAD_DIM] = o
            state_ref[h] = states[h] * un["e_last"] + _dot_tn(un["k_out"], _bf(v_new[h]))
    if rev:
        outs = _headnorm_gate(of_ref[0] + ob_ref[...], nw_ref, gate_ref[0])
        for s, val in enumerate(outs):
            o_ref[0, :, s * LANES:(s + 1) * LANES] = val.astype(o_ref.dtype)


def _gdn_scan(qkv, bg, rev, extra=(), t=256):
    b, s, width = qkv.shape
    n = s // t
    idx = (lambda bi, i: (bi, n - 1 - i, 0)) if rev else (lambda bi, i: (bi, i, 0))
    in_specs = [pl.BlockSpec((1, t, width), idx), pl.BlockSpec((1, t, LANES), idx)]
    scratch = [pltpu.VMEM((C_HEADS, HEAD_DIM, HEAD_DIM), F32)]
    if rev:
        in_specs += [pl.BlockSpec((1, t, C_WIDTH), idx), pl.BlockSpec((1, t, C_WIDTH), idx),
                     pl.BlockSpec((1, LANES), lambda bi, i: (0, 0))]
        scratch.append(pltpu.VMEM((t, C_WIDTH), F32))
    return pl.pallas_call(
        functools.partial(_gdn_kernel, rev),
        grid=(b, n),
        in_specs=in_specs,
        out_specs=pl.BlockSpec((1, t, C_WIDTH), idx),
        out_shape=jax.ShapeDtypeStruct((b, s, C_WIDTH), BF16 if rev else F32),
        scratch_shapes=scratch,
        compiler_params=_cparams("parallel", "arbitrary"),
        name="gdn_bwd" if rev else "gdn_fwd",
    )(qkv, bg, *extra)


def _gdn_mixer(cqkv, cba, gate, conv_w, a_log, dt_bias, norm_w):
    cw = jnp.zeros((HALO_ROWS, cqkv.shape[-1]), F32).at[:CONV_K].set(conv_w)
    par = jnp.zeros((HALO_ROWS, LANES), F32)
    par = par.at[0, 2 * C_HEADS:4 * C_HEADS].set(a_log.reshape(-1)).at[1, 2 * C_HEADS:4 * C_HEADS].set(dt_bias.reshape(-1))
    qkv, bg = _gdn_prep(cqkv, cba, cw, par)
    o_f = _gdn_scan(qkv, bg, False)
    return _gdn_scan(qkv, bg, True, (o_f, gate, jnp.tile(norm_w, 2)[None]))


SUB = 16


def _segment_cumsum(x, seg, rev):
    n = x.shape[0]
    pos = lax.broadcasted_iota(jnp.int32, (n, 1), 0) % seg
    sh = 1
    while sh < seg:
        if rev:
            x = x + jnp.where(pos < seg - sh, pltpu.roll(x, n - sh, 0), 0.0)
        else:
            x = x + jnp.where(pos >= sh, pltpu.roll(x, sh, 0), 0.0)
        sh *= 2
    return x


def _hgrn_kernel(rev, q_ref, v_ref, f_ref, lb_ref, *rest):
    if rev:
        of_ref, gate_ref, nw_ref, o_ref, state_ref, b_ref, k_ref, ob_ref = rest
    else:
        o_ref, state_ref, b_ref, k_ref = rest
        ob_ref = o_ref.at[0]
    t = q_ref.shape[1]
    nsub = t // SUB
    npair = D_WIDTH // LANES

    @pl.when(pl.program_id(1) == 0)
    def _():
        state_ref[...] = jnp.zeros_like(state_ref)

    fl = f_ref[0]
    log_lb, log_1m_lb, one_m_lb = lb_ref[0:1, :], lb_ref[1:2, :], lb_ref[2:3, :]
    y = log_1m_lb + (jnp.minimum(fl, 0.0) - jnp.log1p(jnp.exp(-jnp.abs(fl))))
    mx = jnp.maximum(log_lb, y)
    logf = mx + jnp.log(jnp.exp(log_lb - mx) + jnp.exp(y - mx))
    b_ref[...] = _segment_cumsum(logf, SUB, rev)
    k_ref[...] = one_m_lb * _sigmoid(-fl)

    bd_f = _head_blockdiag(F32)
    bd = _bf(bd_f)
    row = lax.broadcasted_iota(jnp.int32, (SUB, 1), 0)

    def sub_block(it, carry):
        r0 = pl.multiple_of((nsub - 1 - it if rev else it) * SUB, SUB)
        for p in range(npair):
            cols = slice(p * LANES, (p + 1) * LANES)
            b = b_ref[pl.ds(r0, SUB), cols]
            q = q_ref[0, pl.ds(r0, SUB), cols]
            k = k_ref[pl.ds(r0, SUB), cols]
            v = v_ref[0, pl.ds(r0, SUB), cols]
            b_end = b[0:1, :] if rev else b[SUB - 1:SUB, :]
            state_t = state_ref[p]
            o = _dot_nt(_bf(q * jnp.exp(b)), _bf(state_t))
            xs = []
            for j in range(SUB):
                mask = (row <= j) if rev else (row >= j)
                e = jnp.exp(jnp.where(mask, b - b[j:j + 1, :], NEG))
                xs.append(_bf(q * e * k[j:j + 1, :]))
            ys = _dot(jnp.concatenate(xs, axis=0), bd)
            for j in range(SUB):
                o = o + ys[j * SUB:(j + 1) * SUB] * v[j:j + 1, :]
            ob_ref[pl.ds(r0, SUB), cols] = o
            upd = _dot_tn(_bf(v), _bf(k * jnp.exp(b_end - b)))
            state_ref[p] = state_t * jnp.exp(b_end) + upd * bd_f
        return carry

    lax.fori_loop(0, nsub, sub_block, 0)
    if rev:
        outs = _headnorm_gate(of_ref[0] + ob_ref[...], nw_ref, gate_ref[0])
        for s, val in enumerate(outs):
            o_ref[0, :, s * LANES:(s + 1) * LANES] = val.astype(o_ref.dtype)


def _hgrn_scan(q, v, f, lbp, rev, extra=(), t=128):
    b, s, width = q.shape
    n = s // t
    d = 1 if rev else 0
    idx = (lambda bi, i: (bi, n - 1 - i, 0)) if rev else (lambda bi, i: (bi, i, 0))
    fidx = (lambda bi, i: (bi, n - 1 - i, 1)) if rev else (lambda bi, i: (bi, i, 0))
    in_specs = [pl.BlockSpec((1, t, width), idx), pl.BlockSpec((1, t, width), idx), pl.BlockSpec((1, t, width), fidx),
                pl.BlockSpec((HALO_ROWS, width), lambda bi, i: (0, d))]
    scratch = [pltpu.VMEM((width // LANES, LANES, LANES), F32), pltpu.VMEM((t, width), F32), pltpu.VMEM((t, width), F32)]
    if rev:
        in_specs += [pl.BlockSpec((1, t, width), idx), pl.BlockSpec((1, t, width), idx),
                     pl.BlockSpec((1, LANES), lambda bi, i: (0, 0))]
        scratch.append(pltpu.VMEM((t, width), F32))
    return pl.pallas_call(
        functools.partial(_hgrn_kernel, rev),
        grid=(b, n),
        in_specs=in_specs,
        out_specs=pl.BlockSpec((1, t, width), idx),
        out_shape=jax.ShapeDtypeStruct((b, s, width), BF16 if rev else F32),
        scratch_shapes=scratch,
        compiler_params=_cparams("parallel", "arbitrary"),
        name="hgrn_bwd" if rev else "hgrn_fwd",
    )(q, v, f, lbp, *extra)


def _hgrn_mixer(q, v, f, gate, lb, norm_w):
    lbp = jnp.zeros((HALO_ROWS, 2 * D_WIDTH), F32).at[0].set(jnp.log(lb)).at[1].set(jnp.log1p(-lb)).at[2].set(1.0 - lb)
    o_f = _hgrn_scan(q, v, f, lbp, False)
    return _hgrn_scan(q, v, f, lbp, True, (o_f, gate, jnp.tile(norm_w, 2)[None]))


def _hgrn_lower_bound(d_lb, layer):
    lb_cum = jnp.cumsum(jax.nn.softmax(d_lb.astype(F32), axis=0), axis=0)
    return (lb_cum[layer] - lb_cum[0]).reshape(-1)


def _out_proj_kernel(x_ref, oa_ref, ob0_ref, l0_ref, ob1_ref, l1_ref, ob2_ref, l2_ref, oc_ref, od_ref,
                     wa_ref, wb_ref, wc_ref, wd_ref, y_ref):
    lses = [l0_ref[...], l1_ref[...], l2_ref[...]]
    m = jnp.maximum(jnp.maximum(lses[0], lses[1]), lses[2])
    ws = [jnp.exp(l - m) for l in lses]
    ob = (ws[0] * ob0_ref[...] + ws[1] * ob1_ref[...] + ws[2] * ob2_ref[...]) / (ws[0] + ws[1] + ws[2])
    y = x_ref[...] + _dot(oa_ref[...], wa_ref[...]) + _dot(_bf(ob), wb_ref[...])
    y_ref[...] = y + _dot(oc_ref[...], wc_ref[...]) + _dot(od_ref[...], wd_ref[...])


def _out_proj(x2, oa, ob_lse, oc, od, ws, tm=512):
    n = x2.shape[0]
    acts = [x2, oa] + [a for pair in ob_lse for a in pair] + [oc, od]
    return pl.pallas_call(
        _out_proj_kernel,
        grid=(n // tm,),
        in_specs=[_row_spec(tm, a.shape[1]) for a in acts] + [_const_spec(w.shape) for w in ws],
        out_specs=_row_spec(tm, D_MODEL),
        out_shape=jax.ShapeDtypeStruct((n, D_MODEL), F32),
        compiler_params=_cparams("parallel"),
        name="out_proj",
    )(*acts, *ws)


I_PER_STEP = 8
PEER_SQRT_HALF = 0.7071067811865476


def _top_values(x, count):
    vals = []
    for _ in range(count):
        m = jnp.max(x, axis=0, keepdims=True)
        vals.append(m)
        x = jnp.where(x == m, NEG, x)
    return vals


def _peer_stats(x_ref, nw_ref, wq_ref, keys_ref, h_ref, e1_ref, theta_ref, a_ref):
    h = _bf(_rmsnorm_rows(x_ref[...], nw_ref[...]))
    h_ref[...] = h
    q = _dot(h, wq_ref[...])
    ranks = PEER_TOPK + 1
    for hd in range(PEER_HEADS):
        s_t = []
        for p in range(2):
            c0 = (2 * hd + p) * PEER_HALF
            s_t.append(_dot_nt(keys_ref[hd, p], _bf(q[:, c0:c0 + PEER_HALF])))
        top = [_top_values(s, ranks) for s in s_t]
        top1 = jnp.concatenate(top[1], axis=0)
        cand = jnp.concatenate([top[0][i - 1] + top1[:ranks // i] for i in range(1, ranks + 1)], axis=0)
        best = _top_values(cand, ranks)
        z = sum(jnp.exp(b - best[0]) for b in best[:PEER_TOPK])
        thr = 0.5 * (best[PEER_TOPK - 1] + best[PEER_TOPK])
        e1_ref[hd] = jnp.exp(s_t[1] - top[1][0])
        theta_ref[hd] = jnp.exp((thr - top[1][0]) - s_t[0])
        a_ref[hd] = jnp.exp(s_t[0] - top[0][0]) / z


def _peer_weights(jj, u_ref, h_ref, e1_ref, theta_ref, a_ref, w_ref):
    act = _dot_nt(u_ref[...], h_ref[...])
    for ii in range(I_PER_STEP):
        i = jj * I_PER_STEP + ii
        g = None
        for hd in range(PEER_HEADS):
            e1 = e1_ref[hd]
            term = jnp.where(e1 >= theta_ref[hd, pl.ds(i, 1), :], e1, 0.0) * a_ref[hd, pl.ds(i, 1), :]
            g = term if g is None else g + term
        a_ii = act[ii * N_KEYS:(ii + 1) * N_KEYS]
        w_ref[ii * N_KEYS:(ii + 1) * N_KEYS, :] = _bf(g * (0.5 * a_ii * (1.0 + lax.erf(a_ii * PEER_SQRT_HALF))))


def _peer_kernel(final, x_ref, nw_ref, wq_ref, keys_ref, u_ref, vt_ref, fnw_ref, y_ref,
                 h_ref, e1_ref, theta_ref, a_ref, w_ref, acc_ref):
    j = pl.program_id(1)
    last = pl.num_programs(1) - 1
    stats = (e1_ref, theta_ref, a_ref)

    @pl.when(j == 0)
    def _():
        _peer_stats(x_ref, nw_ref, wq_ref, keys_ref, h_ref, *stats)
        _peer_weights(0, u_ref, h_ref, *stats, w_ref.at[0])

    @pl.when(j == 1)
    def _():
        acc_ref[...] = _dot(vt_ref[...], w_ref[0])
        _peer_weights(1, u_ref, h_ref, *stats, w_ref.at[1])

    @pl.when((j > 1) & (j < last))
    def _():
        acc_ref[...] += _dot(vt_ref[...], w_ref[(j - 1) % 2])
        _peer_weights(j, u_ref, h_ref, *stats, w_ref.at[j % 2])

    @pl.when(j == last)
    def _():
        y = x_ref[...] + (acc_ref[...] + _dot(vt_ref[...], w_ref[(last - 1) % 2])).T
        if final:
            y = _rmsnorm_rows(y, fnw_ref[...])
        y_ref[...] = y


def _peer(x2, nw, wq, keys, u, vt, fnw, final, tm=512):
    n = x2.shape[0]
    ec = I_PER_STEP * N_KEYS
    nblk = N_EXPERTS // ec
    tok = lambda i, j: (i, 0)
    const2 = lambda i, j: (0, 0)
    stat = pltpu.VMEM((PEER_HEADS, N_KEYS, tm), F32)
    return pl.pallas_call(
        functools.partial(_peer_kernel, final),
        grid=(n // tm, nblk + 1),
        in_specs=[pl.BlockSpec((tm, D_MODEL), tok), pl.BlockSpec((1, D_MODEL), const2),
                  pl.BlockSpec(wq.shape, const2), pl.BlockSpec(keys.shape, lambda i, j: (0, 0, 0, 0)),
                  pl.BlockSpec((ec, D_MODEL), lambda i, j: (jnp.minimum(j, nblk - 1), 0)),
                  pl.BlockSpec((D_MODEL, ec), lambda i, j: (0, jnp.maximum(j - 1, 0))),
                  pl.BlockSpec((1, D_MODEL), const2)],
        out_specs=pl.BlockSpec((tm, D_MODEL), tok),
        out_shape=jax.ShapeDtypeStruct((n, D_MODEL), F32),
        scratch_shapes=[pltpu.VMEM((tm, D_MODEL), BF16), stat, stat, stat, pltpu.VMEM((2, ec, tm), BF16),
                        pltpu.VMEM((D_MODEL, tm), F32)],
        compiler_params=_cparams("parallel", "arbitrary"),
        name="peer",
    )(x2, nw, wq, keys, u, vt, fnw)


def _split_cols(w):
    parts, off = [], 0
    for width in IN_SPLITS:
        parts.append(w[:, off:off + width])
        off += width
    return parts


def _axial_tables(seq):
    t = jnp.arange(seq)
    half = HEAD_DIM // 4
    inv = AXIAL_THETA ** (-jnp.arange(half, dtype=F32) / half)
    ang_r = (t // GRID_W).astype(F32)[:, None] * inv[None, :]
    ang_c = (t % GRID_W).astype(F32)[:, None] * inv[None, :]
    zero = jnp.zeros_like(ang_r)
    cos = jnp.concatenate([jnp.cos(ang_r), jnp.cos(ang_r), jnp.cos(ang_c), jnp.cos(ang_c)], axis=1)
    sneg = jnp.concatenate([-jnp.sin(ang_r), zero, -jnp.sin(ang_c), zero], axis=1)
    spos = jnp.concatenate([zero, jnp.sin(ang_r), zero, jnp.sin(ang_c)], axis=1)
    return tuple(jnp.tile(a, (1, 2)) for a in (cos, sneg, spos))


def _rope_tables(seq):
    half = ROT_DIMS // 2
    inv = ROPE_THETA ** (-jnp.arange(half, dtype=F32) / half)
    ang = jnp.arange(seq).astype(F32)[:, None] * inv[None, :]
    rest = HEAD_DIM - ROT_DIMS
    cos = jnp.concatenate([jnp.cos(ang), jnp.cos(ang), jnp.ones((seq, rest), F32)], axis=1)
    sneg = jnp.concatenate([-jnp.sin(ang), jnp.zeros((seq, half + rest), F32)], axis=1)
    spos = jnp.concatenate([jnp.zeros((seq, half), F32), jnp.sin(ang), jnp.zeros((seq, rest), F32)], axis=1)
    return tuple(jnp.tile(a, (1, 2)) for a in (cos, sneg, spos))


def _layout_w_in(w):
    (a_q, a_k, a_v, b_q, b_k, b_v, c_qkv, c_beta, c_a, c_gate, d_q, d_i, d_f, d_gate) = _split_cols(w)
    zero = jnp.zeros((D_MODEL, HEAD_DIM), w.dtype)
    q_slabs = []
    for h in range(A_Q_HEADS):
        qh = a_q[:, h * HEAD_DIM:(h + 1) * HEAD_DIM]
        q_slabs += [qh, zero] if h // A_GROUP == 0 else [zero, qh]
    wa = jnp.concatenate(q_slabs + [a_k, a_v], axis=1)
    b_cols = []
    for g in range(len(DIL_CONFIGS)):
        b_cols += [t[:, g * LANES:(g + 1) * LANES] for t in (b_q, b_k, b_v)]
    wb = jnp.concatenate(b_cols, axis=1)
    pad = jnp.zeros((D_MODEL, LANES - 4 * C_HEADS), w.dtype)
    wc = jnp.concatenate([c_qkv, c_gate, c_beta, c_a, pad], axis=1)
    wd = jnp.concatenate([d_q, d_i, d_f, d_gate], axis=1)
    return tuple(_bf(t) for t in (wa, wb, wc, wd))


def _layout_w_out(w):
    bounds = (0, A_WIDTH, A_WIDTH + LANES, A_WIDTH + LANES + C_WIDTH, A_WIDTH + LANES + C_WIDTH + D_WIDTH)
    return tuple(_bf(w[lo:hi]) for lo, hi in zip(bounds[:-1], bounds[1:]))


def _layer_params(l, norm1_w, w_in, a_qnorm_w, a_knorm_w, c_conv_w, c_a_log, c_dt_bias, c_norm_w, d_lb, d_norm_w, w_out,
                  norm2_w, peer_w_query, peer_sub_keys, peer_u, peer_v):
    return dict(
        norm1=norm1_w[l][None], w_in=_layout_w_in(w_in[l]),
        qnw=jnp.tile(a_qnorm_w[l], 2)[None], knw=jnp.tile(a_knorm_w[l], 2)[None],
        conv_w=c_conv_w[l], a_log=c_a_log[l], dt_bias=c_dt_bias[l], c_norm=c_norm_w[l],
        lb=_hgrn_lower_bound(d_lb, l), d_norm=d_norm_w[l], w_out=_layout_w_out(w_out[l]),
        norm2=norm2_w[l][None], wq=_bf(peer_w_query[l]), keys=_bf(peer_sub_keys[l]),
        u=_bf(peer_u[l]), vt=_bf(peer_v[l].T))


def _trunk(x, layers, final_norm_w):
    b, s, _ = x.shape
    n = b * s
    axial, rope = _axial_tables(s), _rope_tables(s)
    x2 = x.reshape(n, D_MODEL)
    seq = lambda t: t.reshape(b, s, t.shape[-1])
    flat = lambda t: t.reshape(n, t.shape[-1])
    for l, p in enumerate(layers):
        wa, wb, wc, wd = p["w_in"]
        qa, ka, va = _proj_a(x2, p["norm1"], wa, p["qnw"], p["knw"], axial, s)
        zb = _proj_b(x2, p["norm1"], wb, rope, s)
        cqkv, cgate, cba = _proj_plain(x2, p["norm1"], wc, (3 * C_WIDTH, C_WIDTH, LANES), (F32, F32, F32), "proj_c")
        dq, di, df, dgate = _proj_plain(x2, p["norm1"], wd, (D_WIDTH, D_WIDTH, 2 * D_WIDTH, D_WIDTH), (F32,) * 4, "proj_d")
        o_a = _attn_a(seq(qa), seq(ka), seq(va))
        ob_lse = [tuple(flat(t) for t in _attn_b(seq(zb), g)) for g in range(len(DIL_CONFIGS))]
        o_c = _gdn_mixer(seq(cqkv), seq(cba), seq(cgate), p["conv_w"], p["a_log"], p["dt_bias"], p["c_norm"])
        o_d = _hgrn_mixer(seq(dq), seq(di), seq(df), seq(dgate), p["lb"], p["d_norm"])
        x2 = _out_proj(x2, flat(o_a), ob_lse, flat(o_c), flat(o_d), p["w_out"])
        x2 = _peer(x2, p["norm2"], p["wq"], p["keys"], p["u"], p["vt"], final_norm_w[None], l == len(layers) - 1)
    return x2.reshape(b, s, D_MODEL)


def kernel(x_prompt, x_sample, norm1_w, w_in, a_qnorm_w, a_knorm_w, c_conv_w, c_a_log, c_dt_bias, c_norm_w, d_lb, d_norm_w, w_out, norm2_w, peer_w_query, peer_sub_keys, peer_u, peer_v, final_norm_w):
    layers = [_layer_params(l, norm1_w, w_in, a_qnorm_w, a_knorm_w, c_conv_w, c_a_log, c_dt_bias, c_norm_w, d_lb, d_norm_w,
                            w_out, norm2_w, peer_w_query, peer_sub_keys, peer_u, peer_v) for l in range(DEPTH)]
    return (_trunk(x_prompt, layers, final_norm_w), _trunk(x_sample, layers, final_norm_w))
```

```python
import functools

import jax
import jax.numpy as jnp
from jax import lax
from jax.experimental import pallas as pl
from jax.experimental.pallas import tpu as pltpu

F32 = jnp.float32
BF16 = jnp.bfloat16

D_MODEL = 1024
DEPTH = 2
HEAD_DIM = 64
LANES = 128
A_Q_HEADS = 6
A_KV_HEADS = 2
A_GROUP = A_Q_HEADS // A_KV_HEADS
AXIAL_THETA = 10000.0
GRID_W = 64
B_HEADS = 6
DIL_CONFIGS = ((128, 1), (512, 4), (2048, 16))
ROPE_THETA = 500000.0
ROT_DIMS = HEAD_DIM // 4
C_HEADS = 6
CONV_K = 5
CHUNK = 64
D_HEADS = 6
N_KEYS = 128
N_EXPERTS = N_KEYS * N_KEYS
PEER_HEADS = 8
PEER_HALF = 128
PEER_TOPK = 16
EPS = 1e-6
NEG = -1e30

A_WIDTH = A_Q_HEADS * HEAD_DIM
A_KV_WIDTH = A_KV_HEADS * HEAD_DIM
B_WIDTH = B_HEADS * HEAD_DIM
C_WIDTH = C_HEADS * HEAD_DIM
D_WIDTH = D_HEADS * HEAD_DIM
IN_SPLITS = (A_WIDTH, A_KV_WIDTH, A_KV_WIDTH, B_WIDTH, B_WIDTH, B_WIDTH, 3 * C_WIDTH, 2 * C_HEADS, 2 * C_HEADS,
             C_WIDTH, D_WIDTH, D_WIDTH, 2 * D_WIDTH, D_WIDTH)

VMEM_LIMIT_BYTES = 48 * 1024 * 1024


def _cparams(*sem):
    return pltpu.CompilerParams(dimension_semantics=sem, vmem_limit_bytes=VMEM_LIMIT_BYTES)


def _bf(x):
    return x.astype(BF16)


def _dot(a, b):
    return jnp.dot(a, b, preferred_element_type=F32)


def _dot_nt(a, b):
    return lax.dot_general(a, b, (((1,), (1,)), ((), ())), preferred_element_type=F32)


def _dot_tn(a, b):
    return lax.dot_general(a, b, (((0,), (0,)), ((), ())), preferred_element_type=F32)


def _split2(x):
    hi = _bf(x)
    lo = _bf(x - hi.astype(F32))
    return hi, lo


def _head_blockdiag(dtype=BF16):
    r = lax.broadcasted_iota(jnp.int32, (LANES, LANES), 0) // HEAD_DIM
    c = lax.broadcasted_iota(jnp.int32, (LANES, LANES), 1) // HEAD_DIM
    return jnp.where(r == c, 1.0, 0.0).astype(dtype)


def _segsum64(x):
    bd = _head_blockdiag()
    hi, lo = _split2(x)
    return _dot(hi, bd) + _dot(lo, bd)


def _rmsnorm_rows(x, w):
    return x * lax.rsqrt(jnp.mean(x * x, axis=-1, keepdims=True) + EPS) * w


def _rope_slab(y, cos, sneg, spos, shift):
    return y * cos + pltpu.roll(y, LANES - shift, 1) * sneg + pltpu.roll(y, shift, 1) * spos


def _sigmoid(x):
    return 1.0 / (1.0 + jnp.exp(-x))


def _silu(x):
    return x * _sigmoid(x)


def _softplus(x):
    return jnp.maximum(x, 0.0) + jnp.log1p(jnp.exp(-jnp.abs(x)))


def _emit_a(h, w_ref, qnw_ref, knw_ref, cos_ref, sneg_ref, spos_ref, q_ref, k_ref, v_ref):
    z = _dot(h, w_ref[...])
    cos, sneg, spos = cos_ref[...], sneg_ref[...], spos_ref[...]

    def norm_rope(zs, nw):
        ms = _segsum64(zs * zs) * (1.0 / HEAD_DIM)
        y = zs * lax.rsqrt(ms + EPS) * nw
        return _rope_slab(y, cos, sneg, spos, HEAD_DIM // 4)

    qnw = qnw_ref[...]
    for s in range(A_Q_HEADS):
        q_ref[:, s * LANES:(s + 1) * LANES] = _bf(norm_rope(z[:, s * LANES:(s + 1) * LANES], qnw) * (HEAD_DIM ** -0.5))
    k_ref[...] = _bf(norm_rope(z[:, 6 * LANES:7 * LANES], knw_ref[...]))
    v_ref[...] = _bf(z[:, 7 * LANES:8 * LANES])


def _emit_b(h, w_ref, cos_ref, sneg_ref, spos_ref, o_ref):
    z = _dot(h, w_ref[...])
    cos, sneg, spos = cos_ref[...], sneg_ref[...], spos_ref[...]
    for g in range(len(DIL_CONFIGS)):
        c0 = 3 * g * LANES
        q = _rope_slab(z[:, c0:c0 + LANES], cos, sneg, spos, ROT_DIMS // 2) * (HEAD_DIM ** -0.5)
        k = _rope_slab(z[:, c0 + LANES:c0 + 2 * LANES], cos, sneg, spos, ROT_DIMS // 2)
        o_ref[:, c0:c0 + LANES] = _bf(q)
        o_ref[:, c0 + LANES:c0 + 2 * LANES] = _bf(k)
        o_ref[:, c0 + 2 * LANES:c0 + 3 * LANES] = _bf(z[:, c0 + 2 * LANES:c0 + 3 * LANES])


def _emit_plain(h, w_ref, o_refs):
    z = _dot(h, w_ref[...])
    off = 0
    for o_ref in o_refs:
        width = o_ref.shape[1]
        o_ref[...] = z[:, off:off + width].astype(o_ref.dtype)
        off += width


C_OUT_WIDTHS = (3 * C_WIDTH, C_WIDTH, LANES)
D_OUT_WIDTHS = (D_WIDTH, D_WIDTH, 2 * D_WIDTH, D_WIDTH)


def _proj_kernel(x_ref, nw_ref, wa_ref, wb_ref, wc_ref, wd_ref, qnw_ref, knw_ref,
                 acos_ref, asneg_ref, aspos_ref, bcos_ref, bsneg_ref, bspos_ref,
                 qa_ref, ka_ref, va_ref, zb_ref, cqkv_ref, cgate_ref, cba_ref, dq_ref, di_ref, df_ref, dgate_ref):
    h = _bf(_rmsnorm_rows(x_ref[...], nw_ref[...]))
    _emit_a(h, wa_ref, qnw_ref, knw_ref, acos_ref, asneg_ref, aspos_ref, qa_ref, ka_ref, va_ref)
    _emit_b(h, wb_ref, bcos_ref, bsneg_ref, bspos_ref, zb_ref)
    _emit_plain(h, wc_ref, (cqkv_ref, cgate_ref, cba_ref))
    _emit_plain(h, wd_ref, (dq_ref, di_ref, df_ref, dgate_ref))


def _row_spec(tm, width):
    return pl.BlockSpec((tm, width), lambda i: (i, 0))


def _const_spec(shape):
    return pl.BlockSpec(shape, lambda i: (0,) * len(shape))


def _proj(x2, nw, ws, qnw, knw, axial, rope, seq, tm=256):
    n = x2.shape[0]
    ns = seq // tm
    tab_spec = pl.BlockSpec((tm, LANES), lambda i: (i % ns, 0))
    outs = [(A_Q_HEADS * LANES, BF16), (LANES, BF16), (LANES, BF16), (ws[1].shape[1], BF16)]
    outs += [(wd, F32) for wd in C_OUT_WIDTHS + D_OUT_WIDTHS]
    return pl.pallas_call(
        _proj_kernel,
        grid=(n // tm,),
        in_specs=[_row_spec(tm, D_MODEL), _const_spec((1, D_MODEL))] + [_const_spec(w.shape) for w in ws]
        + [_const_spec((1, LANES))] * 2 + [tab_spec] * 6,
        out_specs=[_row_spec(tm, wd) for wd, _ in outs],
        out_shape=[jax.ShapeDtypeStruct((n, wd), dt) for wd, dt in outs],
        compiler_params=_cparams("parallel"),
        name="proj",
    )(x2, nw, *ws, qnw, knw, *axial, *rope)


def _lane_is_low():
    return lax.broadcasted_iota(jnp.int32, (1, LANES), 1) < HEAD_DIM


def _attn_a_kernel(tk, q_ref, k_ref, v_ref, o_ref):
    tq = q_ref.shape[1]
    nk = k_ref.shape[1] // tk
    rows = A_GROUP * tq
    low = _lane_is_low()
    per_kv = []
    for kv in range(A_KV_HEADS):
        q = jnp.concatenate([q_ref[0, :, (kv * A_GROUP + g) * LANES:(kv * A_GROUP + g + 1) * LANES]
                             for g in range(A_GROUP)], axis=0)
        own = low if kv == 0 else jnp.logical_not(low)

        def body(c, carry, q=q, own=own):
            m, acc = carry
            start = pl.multiple_of(c * tk, tk)
            s = _dot_nt(q, k_ref[0, pl.ds(start, tk), :])
            m_new = jnp.maximum(m, jnp.max(s, axis=1, keepdims=True))
            alpha = jnp.exp(m - m_new)
            p = jnp.exp((s - m_new).astype(BF16))
            v = v_ref[0, pl.ds(start, tk), :]
            acc = alpha * acc + _dot(p, jnp.where(own, v, jnp.ones_like(v)))
            return m_new, acc

        init = (jnp.full((rows, 1), NEG, F32), jnp.zeros((rows, LANES), F32))
        _, acc = lax.fori_loop(0, nk, body, init)
        per_kv.append(acc / pltpu.roll(acc, HEAD_DIM, 1))
    for j in range(A_Q_HEADS // 2):
        halves = []
        for h in (2 * j, 2 * j + 1):
            kv, g = divmod(h, A_GROUP)
            o = per_kv[kv][g * tq:(g + 1) * tq]
            halves.append(o if kv == h % 2 else pltpu.roll(o, HEAD_DIM, 1))
        o_ref[0, :, j * LANES:(j + 1) * LANES] = jnp.where(low, halves[0], halves[1]).astype(o_ref.dtype)


def _attn_a(q, k, v, tq=256, tk_max=4096):
    b, s, _ = q.shape
    tk = min(s, tk_max)
    return pl.pallas_call(
        functools.partial(_attn_a_kernel, tk),
        grid=(b, s // tq),
        in_specs=[pl.BlockSpec((1, tq, A_Q_HEADS * LANES), lambda bi, i: (bi, i, 0)),
                  pl.BlockSpec((1, s, LANES), lambda bi, i: (bi, 0, 0)),
                  pl.BlockSpec((1, s, LANES), lambda bi, i: (bi, 0, 0))],
        out_specs=pl.BlockSpec((1, tq, A_WIDTH), lambda bi, i: (bi, i, 0)),
        out_shape=jax.ShapeDtypeStruct((b, s, A_WIDTH), BF16),
        compiler_params=_cparams("parallel", "parallel"),
        name="attn_a",
    )(q, k, v)


B_HALF_WINDOW = 64


def _attn_b_kernel(q_ref, k_ref, v_ref, o_ref, lse_ref):
    tq = q_ref.shape[1]
    length = k_ref.shape[1]
    w = B_HALF_WINDOW
    kwid = tq + 2 * w
    l0 = pl.program_id(2) * tq
    ws = pl.multiple_of(jnp.clip(l0 - w, 0, length - kwid), w)
    kw = k_ref[0, pl.ds(ws, kwid), :]
    vw = v_ref[0, pl.ds(ws, kwid), :]
    qpos = l0 + lax.broadcasted_iota(jnp.int32, (tq, 1), 0)
    kpos = ws + lax.broadcasted_iota(jnp.int32, (1, kwid), 1)
    valid = jnp.abs(kpos - qpos) <= w
    q = q_ref[0]
    low = _lane_is_low()
    owns = [low, jnp.logical_not(low)]
    ss = [jnp.where(valid, _dot_nt(jnp.where(own, q, jnp.zeros_like(q)), kw), NEG) for own in owns]
    ms = [jnp.max(s, axis=1, keepdims=True) for s in ss]
    ps = [jnp.exp((s - m).astype(BF16)) for s, m in zip(ss, ms)]
    pvs = [_dot(p, jnp.where(own, vw, jnp.ones_like(vw))) for p, own in zip(ps, owns)]
    zs = [pltpu.roll(pv, HEAD_DIM, 1) for pv in pvs]
    o_ref[0] = jnp.where(low, pvs[0] / zs[0], pvs[1] / zs[1])
    lse_ref[0] = jnp.where(low, ms[0] + jnp.log(zs[0]), ms[1] + jnp.log(zs[1]))


def _attn_b(zb, g):
    b, s, width = zb.shape
    dil = DIL_CONFIGS[g][1]
    assert DIL_CONFIGS[g][0] // (2 * dil) == B_HALF_WINDOW
    length = s // dil
    tq = 256 if length >= 512 else 128
    nslab = width // LANES
    z = zb.reshape(b, length, dil * width)
    out = jax.ShapeDtypeStruct((b, length, dil * LANES), F32)
    o, lse = pl.pallas_call(
        _attn_b_kernel,
        grid=(b, dil, length // tq),
        in_specs=[pl.BlockSpec((1, tq, LANES), lambda bi, r, i: (bi, i, r * nslab + 3 * g)),
                  pl.BlockSpec((1, length, LANES), lambda bi, r, i: (bi, 0, r * nslab + 3 * g + 1)),
                  pl.BlockSpec((1, length, LANES), lambda bi, r, i: (bi, 0, r * nslab + 3 * g + 2))],
        out_specs=[pl.BlockSpec((1, tq, LANES), lambda bi, r, i: (bi, i, r))] * 2,
        out_shape=[out, out],
        compiler_params=_cparams("parallel", "parallel", "parallel"),
        name=f"attn_b{g}",
    )(z, z, z)
    return o.reshape(b, s, LANES), lse.reshape(b, s, LANES)


HALO_ROWS = 8


def _dot_hp(a, b):
    a_hi, a_lo = _split2(a)
    b_hi, b_lo = _split2(b)
    return _dot(a_hi, b_hi) + (_dot(a_hi, b_lo) + _dot(a_lo, b_hi))


def _gdn_prep_kernel(x_ref, prev_ref, next_ref, cw_ref, ba_ref, par_ref, qkv_ref, bg_ref, ext_ref):
    t = x_ref.shape[1]
    i = pl.program_id(1)
    last = pl.num_programs(1) - 1
    ext_ref[0:HALO_ROWS] = jnp.where(i > 0, prev_ref[0], 0.0)
    ext_ref[HALO_ROWS:HALO_ROWS + t] = x_ref[0]
    ext_ref[HALO_ROWS + t:] = jnp.where(i < last, next_ref[0], 0.0)
    acc = None
    for k in range(CONV_K):
        term = ext_ref[pl.ds(HALO_ROWS - CONV_K // 2 + k, t), :] * cw_ref[k:k + 1, :]
        acc = term if acc is None else acc + term
    y = _silu(acc)
    for s in range(3 * C_WIDTH // LANES):
        ys = y[:, s * LANES:(s + 1) * LANES]
        if s < 2 * C_WIDTH // LANES:
            ys = ys * lax.rsqrt(_segsum64(ys * ys) + EPS)
            if s < C_WIDTH // LANES:
                ys = ys * (HEAD_DIM ** -0.5)
        qkv_ref[0, :, s * LANES:(s + 1) * LANES] = ys
    z = ba_ref[0]
    lane = lax.broadcasted_iota(jnp.int32, (1, LANES), 1)
    a_log, dt_bias = par_ref[0:1, :], par_ref[1:2, :]
    g = -jnp.exp(a_log) * _softplus(z + dt_bias)
    bg_ref[0] = jnp.where(lane < 2 * C_HEADS, _sigmoid(z), g)


def _gdn_prep(cqkv, cba, conv_w, par, t=256):
    b, s, width = cqkv.shape
    nh = t // HALO_ROWS
    last_halo = s // HALO_ROWS - 1
    return pl.pallas_call(
        _gdn_prep_kernel,
        grid=(b, s // t),
        in_specs=[pl.BlockSpec((1, t, width), lambda bi, i: (bi, i, 0)),
                  pl.BlockSpec((1, HALO_ROWS, width), lambda bi, i: (bi, jnp.maximum(i * nh - 1, 0), 0)),
                  pl.BlockSpec((1, HALO_ROWS, width), lambda bi, i: (bi, jnp.minimum((i + 1) * nh, last_halo), 0)),
                  pl.BlockSpec(conv_w.shape, lambda bi, i: (0, 0)),
                  pl.BlockSpec((1, t, LANES), lambda bi, i: (bi, i, 0)),
                  pl.BlockSpec(par.shape, lambda bi, i: (0, 0))],
        out_specs=[pl.BlockSpec((1, t, width), lambda bi, i: (bi, i, 0)),
                   pl.BlockSpec((1, t, LANES), lambda bi, i: (bi, i, 0))],
        out_shape=[jax.ShapeDtypeStruct((b, s, width), F32), jax.ShapeDtypeStruct((b, s, LANES), F32)],
        scratch_shapes=[pltpu.VMEM((t + 2 * HALO_ROWS, width), F32)],
        compiler_params=_cparams("parallel", "parallel"),
        name="gdn_prep",
    )(cqkv, cqkv, cqkv, conv_w, cba, par)


def _chunk_cumsum(x, rev):
    n = x.shape[0]
    row = lax.broadcasted_iota(jnp.int32, (n, 1), 0)
    sh = 1
    while sh < n:
        if rev:
            x = x + jnp.where(row < n - sh, pltpu.roll(x, n - sh, 0), 0.0)
        else:
            x = x + jnp.where(row >= sh, pltpu.roll(x, sh, 0), 0.0)
        sh *= 2
    return x


def _headnorm_gate(o, nw_ref, gate):
    outs = []
    for s in range(o.shape[1] // LANES):
        os_ = o[:, s * LANES:(s + 1) * LANES]
        ms = _segsum64(os_ * os_) * (1.0 / HEAD_DIM)
        outs.append(os_ * lax.rsqrt(ms + EPS) * nw_ref[...] * _silu(gate[:, s * LANES:(s + 1) * LANES]))
    return outs


def _gdn_kernel(rev, qkv_ref, bg_ref, *rest):
    if rev:
        of_ref, gate_ref, nw_ref, o_ref, state_ref, ob_ref = rest
    else:
        o_ref, state_ref = rest
        ob_ref = o_ref.at[0]
    t = qkv_ref.shape[1]

    @pl.when(pl.program_id(1) == 0)
    def _():
        state_ref[...] = jnp.zeros_like(state_ref)

    ri = lax.broadcasted_iota(jnp.int32, (CHUNK, CHUNK), 0)
    ci = lax.broadcasted_iota(jnp.int32, (CHUNK, CHUNK), 1)
    incl = (ci >= ri) if rev else (ci <= ri)
    strict = (ci > ri) if rev else (ci < ri)
    chunks = list(range(t // CHUNK))
    if rev:
        chunks.reverse()
    units = {}
    for c in chunks:
        r0 = c * CHUNK
        bg = bg_ref[0, r0:r0 + CHUNK, :]
        gc = _chunk_cumsum(bg, rev)
        gc_t = gc.T
        e_gc = jnp.exp(gc)
        g_last = gc[0:1, :] if rev else gc[CHUNK - 1:CHUNK, :]
        e_last = jnp.exp(g_last)
        e_rel = jnp.exp(g_last - gc)
        for h in range(C_HEADS):
            hb = h + (C_HEADS if rev else 0)
            hg = 2 * C_HEADS + hb
            q = qkv_ref[0, r0:r0 + CHUNK, h * HEAD_DIM:(h + 1) * HEAD_DIM]
            k = qkv_ref[0, r0:r0 + CHUNK, C_WIDTH + h * HEAD_DIM:C_WIDTH + (h + 1) * HEAD_DIM]
            v = qkv_ref[0, r0:r0 + CHUNK, 2 * C_WIDTH + h * HEAD_DIM:2 * C_WIDTH + (h + 1) * HEAD_DIM]
            beta = bg[:, hb:hb + 1]
            decay = jnp.exp(jnp.where(incl, gc[:, hg:hg + 1] - gc_t[hg:hg + 1, :], NEG))
            kb = k * beta
            kq = _dot_nt(_bf(jnp.concatenate([kb, q], axis=0)), _bf(k))
            units[c, h] = dict(
                m=jnp.where(strict, kq[:CHUNK] * decay, 0.0), qk=_bf(kq[CHUNK:] * decay),
                r=jnp.concatenate([v * beta, kb * e_gc[:, hg:hg + 1]], axis=1),
                q_in=_bf(q * e_gc[:, hg:hg + 1]), k_out=_bf(k * e_rel[:, hg:hg + 1]), e_last=e_last[:, hg:hg + 1])
    for un in units.values():
        un["p"] = _bf(un["m"])
        un["r"] = un["r"] - _dot(un["p"], _bf(un["r"]))
    for _ in range(5):
        for un in units.values():
            un["p"] = _bf(_dot(un["p"], un["p"]))
        for un in units.values():
            un["r"] = un["r"] + _dot(un["p"], _bf(un["r"]))
    for c in chunks:
        r0 = c * CHUNK
        states = [state_ref[h] for h in range(C_HEADS)]
        v_new = [units[c, h]["r"][:, :HEAD_DIM] - _dot(_bf(units[c, h]["r"][:, HEAD_DIM:]), _bf(states[h]))
                 for h in range(C_HEADS)]
        for h in range(C_HEADS):
            un = units[c, h]
            o = _dot(un["q_in"], _bf(states[h])) + _dot(un["qk"], _bf(v_new[h]))
            ob_ref[r0:r0 + CHUNK, h * HEAD_DIM:(h + 1) * HEAD_DIM] = o
            state_ref[h] = states[h] * un["e_last"] + _dot_tn(un["k_out"], _bf(v_new[h]))
    if rev:
        outs = _headnorm_gate(of_ref[0] + ob_ref[...], nw_ref, gate_ref[0])
        for s, val in enumerate(outs):
            o_ref[0, :, s * LANES:(s + 1) * LANES] = val.astype(o_ref.dtype)


def _gdn_scan(qkv, bg, rev, extra=(), t=256):
    b, s, width = qkv.shape
    n = s // t
    idx = (lambda bi, i: (bi, n - 1 - i, 0)) if rev else (lambda bi, i: (bi, i, 0))
    in_specs = [pl.BlockSpec((1, t, width), idx), pl.BlockSpec((1, t, LANES), idx)]
    scratch = [pltpu.VMEM((C_HEADS, HEAD_DIM, HEAD_DIM), F32)]
    if rev:
        in_specs += [pl.BlockSpec((1, t, C_WIDTH), idx), pl.BlockSpec((1, t, C_WIDTH), idx),
                     pl.BlockSpec((1, LANES), lambda bi, i: (0, 0))]
        scratch.append(pltpu.VMEM((t, C_WIDTH), F32))
    return pl.pallas_call(
        functools.partial(_gdn_kernel, rev),
        grid=(b, n),
        in_specs=in_specs,
        out_specs=pl.BlockSpec((1, t, C_WIDTH), idx),
        out_shape=jax.ShapeDtypeStruct((b, s, C_WIDTH), BF16 if rev else F32),
        scratch_shapes=scratch,
        compiler_params=_cparams("parallel", "arbitrary"),
        name="gdn_bwd" if rev else "gdn_fwd",
    )(qkv, bg, *extra)


def _gdn_mixer(cqkv, cba, gate, conv_w, a_log, dt_bias, norm_w):
    cw = jnp.zeros((HALO_ROWS, cqkv.shape[-1]), F32).at[:CONV_K].set(conv_w)
    par = jnp.zeros((HALO_ROWS, LANES), F32)
    par = par.at[0, 2 * C_HEADS:4 * C_HEADS].set(a_log.reshape(-1)).at[1, 2 * C_HEADS:4 * C_HEADS].set(dt_bias.reshape(-1))
    qkv, bg = _gdn_prep(cqkv, cba, cw, par)
    o_f = _gdn_scan(qkv, bg, False)
    return _gdn_scan(qkv, bg, True, (o_f, gate, jnp.tile(norm_w, 2)[None]))


SUB = 16


def _segment_cumsum(x, seg, rev):
    n = x.shape[0]
    pos = lax.broadcasted_iota(jnp.int32, (n, 1), 0) % seg
    sh = 1
    while sh < seg:
        if rev:
            x = x + jnp.where(pos < seg - sh, pltpu.roll(x, n - sh, 0), 0.0)
        else:
            x = x + jnp.where(pos >= sh, pltpu.roll(x, sh, 0), 0.0)
        sh *= 2
    return x


def _hgrn_kernel(rev, q_ref, v_ref, f_ref, lb_ref, *rest):
    if rev:
        of_ref, gate_ref, nw_ref, o_ref, state_ref, b_ref, k_ref, ob_ref = rest
    else:
        o_ref, state_ref, b_ref, k_ref = rest
        ob_ref = o_ref.at[0]
    t = q_ref.shape[1]
    nsub = t // SUB
    npair = D_WIDTH // LANES

    @pl.when(pl.program_id(1) == 0)
    def _():
        state_ref[...] = jnp.zeros_like(state_ref)

    fl = f_ref[0]
    log_lb, log_1m_lb, one_m_lb = lb_ref[0:1, :], lb_ref[1:2, :], lb_ref[2:3, :]
    y = log_1m_lb + (jnp.minimum(fl, 0.0) - jnp.log1p(jnp.exp(-jnp.abs(fl))))
    mx = jnp.maximum(log_lb, y)
    logf = mx + jnp.log(jnp.exp(log_lb - mx) + jnp.exp(y - mx))
    b_ref[...] = _segment_cumsum(logf, SUB, rev)
    k_ref[...] = one_m_lb * _sigmoid(-fl)

    bd_f = _head_blockdiag(F32)
    bd = _bf(bd_f)
    row = lax.broadcasted_iota(jnp.int32, (SUB, 1), 0)

    def sub_block(it, carry):
        r0 = pl.multiple_of((nsub - 1 - it if rev else it) * SUB, SUB)
        for p in range(npair):
            cols = slice(p * LANES, (p + 1) * LANES)
            b = b_ref[pl.ds(r0, SUB), cols]
            q = q_ref[0, pl.ds(r0, SUB), cols]
            k = k_ref[pl.ds(r0, SUB), cols]
            v = v_ref[0, pl.ds(r0, SUB), cols]
            b_end = b[0:1, :] if rev else b[SUB - 1:SUB, :]
            state_t = state_ref[p]
            o = _dot_nt(_bf(q * jnp.exp(b)), _bf(state_t))
            xs = []
            for j in range(SUB):
                mask = (row <= j) if rev else (row >= j)
                e = jnp.exp(jnp.where(mask, b - b[j:j + 1, :], NEG))
                xs.append(_bf(q * e * k[j:j + 1, :]))
            ys = _dot(jnp.concatenate(xs, axis=0), bd)
            for j in range(SUB):
                o = o + ys[j * SUB:(j + 1) * SUB] * v[j:j + 1, :]
            ob_ref[pl.ds(r0, SUB), cols] = o
            upd = _dot_tn(_bf(v), _bf(k * jnp.exp(b_end - b)))
            state_ref[p] = state_t * jnp.exp(b_end) + upd * bd_f
        return carry

    lax.fori_loop(0, nsub, sub_block, 0, unroll=True)
    if rev:
        outs = _headnorm_gate(of_ref[0] + ob_ref[...], nw_ref, gate_ref[0])
        for s, val in enumerate(outs):
            o_ref[0, :, s * LANES:(s + 1) * LANES] = val.astype(o_ref.dtype)


def _hgrn_scan(q, v, f, lbp, rev, extra=(), t=128):
    b, s, width = q.shape
    n = s // t
    d = 1 if rev else 0
    idx = (lambda bi, i: (bi, n - 1 - i, 0)) if rev else (lambda bi, i: (bi, i, 0))
    fidx = (lambda bi, i: (bi, n - 1 - i, 1)) if rev else (lambda bi, i: (bi, i, 0))
    in_specs = [pl.BlockSpec((1, t, width), idx), pl.BlockSpec((1, t, width), idx), pl.BlockSpec((1, t, width), fidx),
                pl.BlockSpec((HALO_ROWS, width), lambda bi, i: (0, d))]
    scratch = [pltpu.VMEM((width // LANES, LANES, LANES), F32), pltpu.VMEM((t, width), F32), pltpu.VMEM((t, width), F32)]
    if rev:
        in_specs += [pl.BlockSpec((1, t, width), idx), pl.BlockSpec((1, t, width), idx),
                     pl.BlockSpec((1, LANES), lambda bi, i: (0, 0))]
        scratch.append(pltpu.VMEM((t, width), F32))
    return pl.pallas_call(
        functools.partial(_hgrn_kernel, rev),
        grid=(b, n),
        in_specs=in_specs,
        out_specs=pl.BlockSpec((1, t, width), idx),
        out_shape=jax.ShapeDtypeStruct((b, s, width), BF16 if rev else F32),
        scratch_shapes=scratch,
        compiler_params=_cparams("parallel", "arbitrary"),
        name="hgrn_bwd" if rev else "hgrn_fwd",
    )(q, v, f, lbp, *extra)


def _hgrn_mixer(q, v, f, gate, lb, norm_w):
    lbp = jnp.zeros((HALO_ROWS, 2 * D_WIDTH), F32).at[0].set(jnp.log(lb)).at[1].set(jnp.log1p(-lb)).at[2].set(1.0 - lb)
    o_f = _hgrn_scan(q, v, f, lbp, False)
    return _hgrn_scan(q, v, f, lbp, True, (o_f, gate, jnp.tile(norm_w, 2)[None]))


def _hgrn_lower_bound(d_lb, layer):
    lb_cum = jnp.cumsum(jax.nn.softmax(d_lb.astype(F32), axis=0), axis=0)
    return (lb_cum[layer] - lb_cum[0]).reshape(-1)


def _out_proj_kernel(x_ref, oa_ref, ob0_ref, l0_ref, ob1_ref, l1_ref, ob2_ref, l2_ref, oc_ref, od_ref,
                     wa_ref, wb_ref, wc_ref, wd_ref, y_ref):
    lses = [l0_ref[...], l1_ref[...], l2_ref[...]]
    m = jnp.maximum(jnp.maximum(lses[0], lses[1]), lses[2])
    ws = [jnp.exp(l - m) for l in lses]
    ob = (ws[0] * ob0_ref[...] + ws[1] * ob1_ref[...] + ws[2] * ob2_ref[...]) / (ws[0] + ws[1] + ws[2])
    y = x_ref[...] + _dot(oa_ref[...], wa_ref[...]) + _dot(_bf(ob), wb_ref[...])
    y_ref[...] = y + _dot(oc_ref[...], wc_ref[...]) + _dot(od_ref[...], wd_ref[...])


def _out_proj(x2, oa, ob_lse, oc, od, ws, tm=512):
    n = x2.shape[0]
    acts = [x2, oa] + [a for pair in ob_lse for a in pair] + [oc, od]
    return pl.pallas_call(
        _out_proj_kernel,
        grid=(n // tm,),
        in_specs=[_row_spec(tm, a.shape[1]) for a in acts] + [_const_spec(w.shape) for w in ws],
        out_specs=_row_spec(tm, D_MODEL),
        out_shape=jax.ShapeDtypeStruct((n, D_MODEL), F32),
        compiler_params=_cparams("parallel"),
        name="out_proj",
    )(*acts, *ws)


I_PER_STEP = 8
PEER_SQRT_HALF = 0.7071067811865476


def _top_values(x, count):
    vals = []
    for _ in range(count):
        m = jnp.max(x, axis=0, keepdims=True)
        vals.append(m)
        x = jnp.where(x == m, NEG, x)
    return vals


def _peer_stats(x_ref, nw_ref, wq_ref, keys_ref, h_ref, e1_ref, theta_ref, a_ref):
    h = _bf(_rmsnorm_rows(x_ref[...], nw_ref[...]))
    h_ref[...] = h
    q = _dot(h, wq_ref[...])
    ranks = PEER_TOPK + 1
    for hd in range(PEER_HEADS):
        s_t = []
        for p in range(2):
            c0 = (2 * hd + p) * PEER_HALF
            s_t.append(_dot_nt(keys_ref[hd, p], _bf(q[:, c0:c0 + PEER_HALF])))
        top = [_top_values(s, ranks) for s in s_t]
        top1 = jnp.concatenate(top[1], axis=0)
        cand = jnp.concatenate([top[0][i - 1] + top1[:ranks // i] for i in range(1, ranks + 1)], axis=0)
        best = _top_values(cand, ranks)
        z = sum(jnp.exp(b - best[0]) for b in best[:PEER_TOPK])
        thr = 0.5 * (best[PEER_TOPK - 1] + best[PEER_TOPK])
        e1_ref[hd] = jnp.exp(s_t[1] - top[1][0])
        theta_ref[hd] = jnp.exp((thr - top[1][0]) - s_t[0])
        a_ref[hd] = jnp.exp(s_t[0] - top[0][0]) * (0.5 / z)


def _peer_weights(jj, u_ref, h_ref, e1_ref, theta_ref, a_ref, w_ref):
    act = _dot_nt(u_ref[...], h_ref[...])
    for ii in range(I_PER_STEP):
        i = jj * I_PER_STEP + ii
        g = None
        for hd in range(PEER_HEADS):
            e1 = e1_ref[hd]
            term = jnp.where(e1 >= theta_ref[hd, pl.ds(i, 1), :], e1, 0.0) * a_ref[hd, pl.ds(i, 1), :]
            g = term if g is None else g + term
        a_ii = act[ii * N_KEYS:(ii + 1) * N_KEYS]
        w_ref[ii * N_KEYS:(ii + 1) * N_KEYS, :] = _bf(g * (a_ii + a_ii * lax.erf(a_ii * PEER_SQRT_HALF)))


def _peer_kernel(final, x_ref, nw_ref, wq_ref, keys_ref, u_ref, vt_ref, fnw_ref, y_ref,
                 h_ref, e1_ref, theta_ref, a_ref, w_ref, acc_ref):
    j = pl.program_id(1)
    last = pl.num_programs(1) - 1
    stats = (e1_ref, theta_ref, a_ref)

    @pl.when(j == 0)
    def _():
        _peer_stats(x_ref, nw_ref, wq_ref, keys_ref, h_ref, *stats)
        _peer_weights(0, u_ref, h_ref, *stats, w_ref.at[0])

    @pl.when(j == 1)
    def _():
        acc_ref[...] = _dot(vt_ref[...], w_ref[0])
        _peer_weights(1, u_ref, h_ref, *stats, w_ref.at[1])

    @pl.when((j > 1) & (j < last))
    def _():
        acc_ref[...] += _dot(vt_ref[...], w_ref[(j - 1) % 2])
        _peer_weights(j, u_ref, h_ref, *stats, w_ref.at[j % 2])

    @pl.when(j == last)
    def _():
        y = x_ref[...] + (acc_ref[...] + _dot(vt_ref[...], w_ref[(last - 1) % 2])).T
        if final:
            y = _rmsnorm_rows(y, fnw_ref[...])
        y_ref[...] = y


def _peer(x2, nw, wq, keys, u, vt, fnw, final, tm=512):
    n = x2.shape[0]
    ec = I_PER_STEP * N_KEYS
    nblk = N_EXPERTS // ec
    tok = lambda i, j: (i, 0)
    const2 = lambda i, j: (0, 0)
    stat = pltpu.VMEM((PEER_HEADS, N_KEYS, tm), F32)
    return pl.pallas_call(
        functools.partial(_peer_kernel, final),
        grid=(n // tm, nblk + 1),
        in_specs=[pl.BlockSpec((tm, D_MODEL), tok), pl.BlockSpec((1, D_MODEL), const2),
                  pl.BlockSpec(wq.shape, const2), pl.BlockSpec(keys.shape, lambda i, j: (0, 0, 0, 0)),
                  pl.BlockSpec((ec, D_MODEL), lambda i, j: (jnp.minimum(j, nblk - 1), 0)),
                  pl.BlockSpec((D_MODEL, ec), lambda i, j: (0, jnp.maximum(j - 1, 0))),
                  pl.BlockSpec((1, D_MODEL), const2)],
        out_specs=pl.BlockSpec((tm, D_MODEL), tok),
        out_shape=jax.ShapeDtypeStruct((n, D_MODEL), F32),
        scratch_shapes=[pltpu.VMEM((tm, D_MODEL), BF16), stat, stat, stat, pltpu.VMEM((2, ec, tm), BF16),
                        pltpu.VMEM((D_MODEL, tm), F32)],
        compiler_params=_cparams("parallel", "arbitrary"),
        name="peer",
    )(x2, nw, wq, keys, u, vt, fnw)


def _split_cols(w):
    parts, off = [], 0
    for width in IN_SPLITS:
        parts.append(w[:, off:off + width])
        off += width
    return parts


def _axial_tables(seq):
    t = jnp.arange(seq)
    half = HEAD_DIM // 4
    inv = AXIAL_THETA ** (-jnp.arange(half, dtype=F32) / half)
    ang_r = (t // GRID_W).astype(F32)[:, None] * inv[None, :]
    ang_c = (t % GRID_W).astype(F32)[:, None] * inv[None, :]
    zero = jnp.zeros_like(ang_r)
    cos = jnp.concatenate([jnp.cos(ang_r), jnp.cos(ang_r), jnp.cos(ang_c), jnp.cos(ang_c)], axis=1)
    sneg = jnp.concatenate([-jnp.sin(ang_r), zero, -jnp.sin(ang_c), zero], axis=1)
    spos = jnp.concatenate([zero, jnp.sin(ang_r), zero, jnp.sin(ang_c)], axis=1)
    return tuple(jnp.tile(a, (1, 2)) for a in (cos, sneg, spos))


def _rope_tables(seq):
    half = ROT_DIMS // 2
    inv = ROPE_THETA ** (-jnp.arange(half, dtype=F32) / half)
    ang = jnp.arange(seq).astype(F32)[:, None] * inv[None, :]
    rest = HEAD_DIM - ROT_DIMS
    cos = jnp.concatenate([jnp.cos(ang), jnp.cos(ang), jnp.ones((seq, rest), F32)], axis=1)
    sneg = jnp.concatenate([-jnp.sin(ang), jnp.zeros((seq, half + rest), F32)], axis=1)
    spos = jnp.concatenate([jnp.zeros((seq, half), F32), jnp.sin(ang), jnp.zeros((seq, rest), F32)], axis=1)
    return tuple(jnp.tile(a, (1, 2)) for a in (cos, sneg, spos))


def _layout_w_in(w):
    (a_q, a_k, a_v, b_q, b_k, b_v, c_qkv, c_beta, c_a, c_gate, d_q, d_i, d_f, d_gate) = _split_cols(w)
    zero = jnp.zeros((D_MODEL, HEAD_DIM), w.dtype)
    q_slabs = []
    for h in range(A_Q_HEADS):
        qh = a_q[:, h * HEAD_DIM:(h + 1) * HEAD_DIM]
        q_slabs += [qh, zero] if h // A_GROUP == 0 else [zero, qh]
    wa = jnp.concatenate(q_slabs + [a_k, a_v], axis=1)
    b_cols = []
    for g in range(len(DIL_CONFIGS)):
        b_cols += [t[:, g * LANES:(g + 1) * LANES] for t in (b_q, b_k, b_v)]
    wb = jnp.concatenate(b_cols, axis=1)
    pad = jnp.zeros((D_MODEL, LANES - 4 * C_HEADS), w.dtype)
    wc = jnp.concatenate([c_qkv, c_gate, c_beta, c_a, pad], axis=1)
    wd = jnp.concatenate([d_q, d_i, d_f, d_gate], axis=1)
    return tuple(_bf(t) for t in (wa, wb, wc, wd))


def _layout_w_out(w):
    bounds = (0, A_WIDTH, A_WIDTH + LANES, A_WIDTH + LANES + C_WIDTH, A_WIDTH + LANES + C_WIDTH + D_WIDTH)
    return tuple(_bf(w[lo:hi]) for lo, hi in zip(bounds[:-1], bounds[1:]))


def _layer_params(l, norm1_w, w_in, a_qnorm_w, a_knorm_w, c_conv_w, c_a_log, c_dt_bias, c_norm_w, d_lb, d_norm_w, w_out,
                  norm2_w, peer_w_query, peer_sub_keys, peer_u, peer_v):
    return dict(
        norm1=norm1_w[l][None], w_in=_layout_w_in(w_in[l]),
        qnw=jnp.tile(a_qnorm_w[l], 2)[None], knw=jnp.tile(a_knorm_w[l], 2)[None],
        conv_w=c_conv_w[l], a_log=c_a_log[l], dt_bias=c_dt_bias[l], c_norm=c_norm_w[l],
        lb=_hgrn_lower_bound(d_lb, l), d_norm=d_norm_w[l], w_out=_layout_w_out(w_out[l]),
        norm2=norm2_w[l][None], wq=_bf(peer_w_query[l]), keys=_bf(peer_sub_keys[l]),
        u=_bf(peer_u[l]), vt=_bf(peer_v[l].T))


def _trunk(x, layers, final_norm_w):
    b, s, _ = x.shape
    n = b * s
    axial, rope = _axial_tables(s), _rope_tables(s)
    x2 = x.reshape(n, D_MODEL)
    seq = lambda t: t.reshape(b, s, t.shape[-1])
    flat = lambda t: t.reshape(n, t.shape[-1])
    for l, p in enumerate(layers):
        qa, ka, va, zb, cqkv, cgate, cba, dq, di, df, dgate = _proj(
            x2, p["norm1"], p["w_in"], p["qnw"], p["knw"], axial, rope, s)
        o_a = _attn_a(seq(qa), seq(ka), seq(va))
        ob_lse = [tuple(flat(t) for t in _attn_b(seq(zb), g)) for g in range(len(DIL_CONFIGS))]
        o_c = _gdn_mixer(seq(cqkv), seq(cba), seq(cgate), p["conv_w"], p["a_log"], p["dt_bias"], p["c_norm"])
        o_d = _hgrn_mixer(seq(dq), seq(di), seq(df), seq(dgate), p["lb"], p["d_norm"])
        x2 = _out_proj(x2, flat(o_a), ob_lse, flat(o_c), flat(o_d), p["w_out"])
        x2 = _peer(x2, p["norm2"], p["wq"], p["keys"], p["u"], p["vt"], final_norm_w[None], l == len(layers) - 1)
    return x2.reshape(b, s, D_MODEL)


def kernel(x_prompt, x_sample, norm1_w, w_in, a_qnorm_w, a_knorm_w, c_conv_w, c_a_log, c_dt_bias, c_norm_w, d_lb, d_norm_w, w_out, norm2_w, peer_w_query, peer_sub_keys, peer_u, peer_v, final_norm_w):
    layers = [_layer_params(l, norm1_w, w_in, a_qnorm_w, a_knorm_w, c_conv_w, c_a_log, c_dt_bias, c_norm_w, d_lb, d_norm_w,
                            w_out, norm2_w, peer_w_query, peer_sub_keys, peer_u, peer_v) for l in range(DEPTH)]
    return (_trunk(x_prompt, layers, final_norm_w), _trunk(x_sample, layers, final_norm_w))
```

```python
import functools

import jax
import jax.numpy as jnp
from jax import lax
from jax.experimental import pallas as pl
from jax.experimental.pallas import tpu as pltpu

F32 = jnp.float32
BF16 = jnp.bfloat16

D_MODEL = 1024
DEPTH = 2
HEAD_DIM = 64
LANES = 128
A_Q_HEADS = 6
A_KV_HEADS = 2
A_GROUP = A_Q_HEADS // A_KV_HEADS
AXIAL_THETA = 10000.0
GRID_W = 64
B_HEADS = 6
DIL_CONFIGS = ((128, 1), (512, 4), (2048, 16))
ROPE_THETA = 500000.0
ROT_DIMS = HEAD_DIM // 4
C_HEADS = 6
CONV_K = 5
CHUNK = 64
D_HEADS = 6
N_KEYS = 128
N_EXPERTS = N_KEYS * N_KEYS
PEER_HEADS = 8
PEER_HALF = 128
PEER_TOPK = 16
EPS = 1e-6
NEG = -1e30

A_WIDTH = A_Q_HEADS * HEAD_DIM
A_KV_WIDTH = A_KV_HEADS * HEAD_DIM
B_WIDTH = B_HEADS * HEAD_DIM
C_WIDTH = C_HEADS * HEAD_DIM
D_WIDTH = D_HEADS * HEAD_DIM
IN_SPLITS = (A_WIDTH, A_KV_WIDTH, A_KV_WIDTH, B_WIDTH, B_WIDTH, B_WIDTH, 3 * C_WIDTH, 2 * C_HEADS, 2 * C_HEADS,
             C_WIDTH, D_WIDTH, D_WIDTH, 2 * D_WIDTH, D_WIDTH)

VMEM_LIMIT_BYTES = 48 * 1024 * 1024


def _cparams(*sem):
    return pltpu.CompilerParams(dimension_semantics=sem, vmem_limit_bytes=VMEM_LIMIT_BYTES)


def _bf(x):
    return x.astype(BF16)


def _dot(a, b):
    return jnp.dot(a, b, preferred_element_type=F32)


def _dot_nt(a, b):
    return lax.dot_general(a, b, (((1,), (1,)), ((), ())), preferred_element_type=F32)


def _dot_tn(a, b):
    return lax.dot_general(a, b, (((0,), (0,)), ((), ())), preferred_element_type=F32)


def _split2(x):
    hi = _bf(x)
    lo = _bf(x - hi.astype(F32))
    return hi, lo


def _head_blockdiag(dtype=BF16):
    r = lax.broadcasted_iota(jnp.int32, (LANES, LANES), 0) // HEAD_DIM
    c = lax.broadcasted_iota(jnp.int32, (LANES, LANES), 1) // HEAD_DIM
    return jnp.where(r == c, 1.0, 0.0).astype(dtype)


def _segsum64(x):
    bd = _head_blockdiag()
    hi, lo = _split2(x)
    return _dot(hi, bd) + _dot(lo, bd)


def _rmsnorm_rows(x, w):
    return x * lax.rsqrt(jnp.mean(x * x, axis=-1, keepdims=True) + EPS) * w


def _rope_slab(y, cos, sneg, spos, shift):
    return y * cos + pltpu.roll(y, LANES - shift, 1) * sneg + pltpu.roll(y, shift, 1) * spos


def _sigmoid(x):
    return 1.0 / (1.0 + jnp.exp(-x))


def _silu(x):
    return x * _sigmoid(x)


def _softplus(x):
    return jnp.maximum(x, 0.0) + jnp.log1p(jnp.exp(-jnp.abs(x)))


def _emit_a(h, w_ref, qnw_ref, knw_ref, cos_ref, sneg_ref, spos_ref, q_ref, k_ref, v_ref):
    z = _dot(h, w_ref[...])
    cos, sneg, spos = cos_ref[...], sneg_ref[...], spos_ref[...]

    def norm_rope(zs, nw):
        ms = _segsum64(zs * zs) * (1.0 / HEAD_DIM)
        y = zs * lax.rsqrt(ms + EPS) * nw
        return _rope_slab(y, cos, sneg, spos, HEAD_DIM // 4)

    qnw = qnw_ref[...]
    for s in range(A_Q_HEADS):
        q_ref[:, s * LANES:(s + 1) * LANES] = _bf(norm_rope(z[:, s * LANES:(s + 1) * LANES], qnw) * (HEAD_DIM ** -0.5))
    k_ref[...] = _bf(norm_rope(z[:, 6 * LANES:7 * LANES], knw_ref[...]))
    v_ref[...] = _bf(z[:, 7 * LANES:8 * LANES])


def _emit_b(h, w_ref, cos_ref, sneg_ref, spos_ref, o_ref):
    z = _dot(h, w_ref[...])
    cos, sneg, spos = cos_ref[...], sneg_ref[...], spos_ref[...]
    for g in range(len(DIL_CONFIGS)):
        c0 = 3 * g * LANES
        q = _rope_slab(z[:, c0:c0 + LANES], cos, sneg, spos, ROT_DIMS // 2) * (HEAD_DIM ** -0.5)
        k = _rope_slab(z[:, c0 + LANES:c0 + 2 * LANES], cos, sneg, spos, ROT_DIMS // 2)
        o_ref[:, c0:c0 + LANES] = _bf(q)
        o_ref[:, c0 + LANES:c0 + 2 * LANES] = _bf(k)
        o_ref[:, c0 + 2 * LANES:c0 + 3 * LANES] = _bf(z[:, c0 + 2 * LANES:c0 + 3 * LANES])


def _emit_plain(h, w_ref, o_refs):
    z = _dot(h, w_ref[...])
    off = 0
    for o_ref in o_refs:
        width = o_ref.shape[1]
        o_ref[...] = z[:, off:off + width].astype(o_ref.dtype)
        off += width


C_OUT_WIDTHS = (3 * C_WIDTH, C_WIDTH, LANES)
D_OUT_WIDTHS = (D_WIDTH, D_WIDTH, 2 * D_WIDTH, D_WIDTH)


def _proj_kernel(x_ref, nw_ref, wa_ref, wb_ref, wc_ref, wd_ref, qnw_ref, knw_ref,
                 acos_ref, asneg_ref, aspos_ref, bcos_ref, bsneg_ref, bspos_ref,
                 qa_ref, ka_ref, va_ref, zb_ref, cqkv_ref, cgate_ref, cba_ref, dq_ref, di_ref, df_ref, dgate_ref):
    h = _bf(_rmsnorm_rows(x_ref[...], nw_ref[...]))
    _emit_a(h, wa_ref, qnw_ref, knw_ref, acos_ref, asneg_ref, aspos_ref, qa_ref, ka_ref, va_ref)
    _emit_b(h, wb_ref, bcos_ref, bsneg_ref, bspos_ref, zb_ref)
    _emit_plain(h, wc_ref, (cqkv_ref, cgate_ref, cba_ref))
    _emit_plain(h, wd_ref, (dq_ref, di_ref, df_ref, dgate_ref))


def _row_spec(tm, width):
    return pl.BlockSpec((tm, width), lambda i: (i, 0))


def _const_spec(shape):
    return pl.BlockSpec(shape, lambda i: (0,) * len(shape))


def _proj(x2, nw, ws, qnw, knw, axial, rope, seq, tm=256):
    n = x2.shape[0]
    ns = seq // tm
    tab_spec = pl.BlockSpec((tm, LANES), lambda i: (i % ns, 0))
    outs = [(A_Q_HEADS * LANES, BF16), (LANES, BF16), (LANES, BF16), (ws[1].shape[1], BF16)]
    outs += [(wd, F32) for wd in C_OUT_WIDTHS + D_OUT_WIDTHS]
    return pl.pallas_call(
        _proj_kernel,
        grid=(n // tm,),
        in_specs=[_row_spec(tm, D_MODEL), _const_spec((1, D_MODEL))] + [_const_spec(w.shape) for w in ws]
        + [_const_spec((1, LANES))] * 2 + [tab_spec] * 6,
        out_specs=[_row_spec(tm, wd) for wd, _ in outs],
        out_shape=[jax.ShapeDtypeStruct((n, wd), dt) for wd, dt in outs],
        compiler_params=_cparams("parallel"),
        name="proj",
    )(x2, nw, *ws, qnw, knw, *axial, *rope)


def _lane_is_low():
    return lax.broadcasted_iota(jnp.int32, (1, LANES), 1) < HEAD_DIM


def _attn_a_kernel(tk, q_ref, k_ref, v_ref, o_ref):
    tq = q_ref.shape[1]
    nk = k_ref.shape[1] // tk
    rows = A_GROUP * tq
    low = _lane_is_low()
    per_kv = []
    for kv in range(A_KV_HEADS):
        q = jnp.concatenate([q_ref[0, :, (kv * A_GROUP + g) * LANES:(kv * A_GROUP + g + 1) * LANES]
                             for g in range(A_GROUP)], axis=0)
        own = low if kv == 0 else jnp.logical_not(low)

        def body(c, carry, q=q, own=own):
            m, acc = carry
            start = pl.multiple_of(c * tk, tk)
            s = _dot_nt(q, k_ref[0, pl.ds(start, tk), :])
            m_new = jnp.maximum(m, jnp.max(s, axis=1, keepdims=True))
            alpha = jnp.exp(m - m_new)
            p = jnp.exp((s - m_new).astype(BF16))
            v = v_ref[0, pl.ds(start, tk), :]
            acc = alpha * acc + _dot(p, jnp.where(own, v, jnp.ones_like(v)))
            return m_new, acc

        init = (jnp.full((rows, 1), NEG, F32), jnp.zeros((rows, LANES), F32))
        _, acc = lax.fori_loop(0, nk, body, init)
        per_kv.append(acc / pltpu.roll(acc, HEAD_DIM, 1))
    for j in range(A_Q_HEADS // 2):
        halves = []
        for h in (2 * j, 2 * j + 1):
            kv, g = divmod(h, A_GROUP)
            o = per_kv[kv][g * tq:(g + 1) * tq]
            halves.append(o if kv == h % 2 else pltpu.roll(o, HEAD_DIM, 1))
        o_ref[0, :, j * LANES:(j + 1) * LANES] = jnp.where(low, halves[0], halves[1]).astype(o_ref.dtype)


def _attn_a(q, k, v, tq=256, tk_max=4096):
    b, s, _ = q.shape
    tk = min(s, tk_max)
    return pl.pallas_call(
        functools.partial(_attn_a_kernel, tk),
        grid=(b, s // tq),
        in_specs=[pl.BlockSpec((1, tq, A_Q_HEADS * LANES), lambda bi, i: (bi, i, 0)),
                  pl.BlockSpec((1, s, LANES), lambda bi, i: (bi, 0, 0)),
                  pl.BlockSpec((1, s, LANES), lambda bi, i: (bi, 0, 0))],
        out_specs=pl.BlockSpec((1, tq, A_WIDTH), lambda bi, i: (bi, i, 0)),
        out_shape=jax.ShapeDtypeStruct((b, s, A_WIDTH), BF16),
        compiler_params=_cparams("parallel", "parallel"),
        name="attn_a",
    )(q, k, v)


B_HALF_WINDOW = 64


def _attn_b_kernel(q_ref, k_ref, v_ref, o_ref, lse_ref):
    tq = q_ref.shape[1]
    length = k_ref.shape[1]
    w = B_HALF_WINDOW
    kwid = tq + 2 * w
    l0 = pl.program_id(2) * tq
    ws = pl.multiple_of(jnp.clip(l0 - w, 0, length - kwid), w)
    kw = k_ref[0, pl.ds(ws, kwid), :]
    vw = v_ref[0, pl.ds(ws, kwid), :]
    qpos = l0 + lax.broadcasted_iota(jnp.int32, (tq, 1), 0)
    kpos = ws + lax.broadcasted_iota(jnp.int32, (1, kwid), 1)
    valid = jnp.abs(kpos - qpos) <= w
    q = q_ref[0]
    low = _lane_is_low()
    owns = [low, jnp.logical_not(low)]
    ss = [jnp.where(valid, _dot_nt(jnp.where(own, q, jnp.zeros_like(q)), kw), NEG) for own in owns]
    ms = [jnp.max(s, axis=1, keepdims=True) for s in ss]
    ps = [jnp.exp((s - m).astype(BF16)) for s, m in zip(ss, ms)]
    pvs = [_dot(p, jnp.where(own, vw, jnp.ones_like(vw))) for p, own in zip(ps, owns)]
    zs = [pltpu.roll(pv, HEAD_DIM, 1) for pv in pvs]
    o_ref[0] = jnp.where(low, pvs[0] / zs[0], pvs[1] / zs[1])
    lse_ref[0] = jnp.where(low, ms[0] + jnp.log(zs[0]), ms[1] + jnp.log(zs[1]))


def _attn_b(zb, g):
    b, s, width = zb.shape
    dil = DIL_CONFIGS[g][1]
    assert DIL_CONFIGS[g][0] // (2 * dil) == B_HALF_WINDOW
    length = s // dil
    tq = 256 if length >= 512 else 128
    if dil == 1:
        z, nslab, base = zb, width // LANES, 3 * g
    else:
        z, nslab, base = zb[:, :, 3 * g * LANES:3 * (g + 1) * LANES].reshape(b, length, dil * 3 * LANES), 3, 0
    out = jax.ShapeDtypeStruct((b, length, dil * LANES), F32)
    o, lse = pl.pallas_call(
        _attn_b_kernel,
        grid=(b, dil, length // tq),
        in_specs=[pl.BlockSpec((1, tq, LANES), lambda bi, r, i: (bi, i, r * nslab + base)),
                  pl.BlockSpec((1, length, LANES), lambda bi, r, i: (bi, 0, r * nslab + base + 1)),
                  pl.BlockSpec((1, length, LANES), lambda bi, r, i: (bi, 0, r * nslab + base + 2))],
        out_specs=[pl.BlockSpec((1, tq, LANES), lambda bi, r, i: (bi, i, r))] * 2,
        out_shape=[out, out],
        compiler_params=_cparams("parallel", "parallel", "parallel"),
        name=f"attn_b{g}",
    )(z, z, z)
    return o.reshape(b, s, LANES), lse.reshape(b, s, LANES)


HALO_ROWS = 8


def _dot_hp(a, b):
    a_hi, a_lo = _split2(a)
    b_hi, b_lo = _split2(b)
    return _dot(a_hi, b_hi) + (_dot(a_hi, b_lo) + _dot(a_lo, b_hi))


def _gdn_prep_kernel(x_ref, prev_ref, next_ref, cw_ref, ba_ref, par_ref, qkv_ref, bg_ref, ext_ref):
    t = x_ref.shape[1]
    i = pl.program_id(1)
    last = pl.num_programs(1) - 1
    ext_ref[0:HALO_ROWS] = jnp.where(i > 0, prev_ref[0], 0.0)
    ext_ref[HALO_ROWS:HALO_ROWS + t] = x_ref[0]
    ext_ref[HALO_ROWS + t:] = jnp.where(i < last, next_ref[0], 0.0)
    acc = None
    for k in range(CONV_K):
        term = ext_ref[pl.ds(HALO_ROWS - CONV_K // 2 + k, t), :] * cw_ref[k:k + 1, :]
        acc = term if acc is None else acc + term
    y = _silu(acc)
    for s in range(3 * C_WIDTH // LANES):
        ys = y[:, s * LANES:(s + 1) * LANES]
        if s < 2 * C_WIDTH // LANES:
            ys = ys * lax.rsqrt(_segsum64(ys * ys) + EPS)
            if s < C_WIDTH // LANES:
                ys = ys * (HEAD_DIM ** -0.5)
        qkv_ref[0, :, s * LANES:(s + 1) * LANES] = ys
    z = ba_ref[0]
    lane = lax.broadcasted_iota(jnp.int32, (1, LANES), 1)
    a_log, dt_bias = par_ref[0:1, :], par_ref[1:2, :]
    g = -jnp.exp(a_log) * _softplus(z + dt_bias)
    bg_ref[0] = jnp.where(lane < 2 * C_HEADS, _sigmoid(z), g)


def _gdn_prep(cqkv, cba, conv_w, par, t=256):
    b, s, width = cqkv.shape
    nh = t // HALO_ROWS
    last_halo = s // HALO_ROWS - 1
    return pl.pallas_call(
        _gdn_prep_kernel,
        grid=(b, s // t),
        in_specs=[pl.BlockSpec((1, t, width), lambda bi, i: (bi, i, 0)),
                  pl.BlockSpec((1, HALO_ROWS, width), lambda bi, i: (bi, jnp.maximum(i * nh - 1, 0), 0)),
                  pl.BlockSpec((1, HALO_ROWS, width), lambda bi, i: (bi, jnp.minimum((i + 1) * nh, last_halo), 0)),
                  pl.BlockSpec(conv_w.shape, lambda bi, i: (0, 0)),
                  pl.BlockSpec((1, t, LANES), lambda bi, i: (bi, i, 0)),
                  pl.BlockSpec(par.shape, lambda bi, i: (0, 0))],
        out_specs=[pl.BlockSpec((1, t, width), lambda bi, i: (bi, i, 0)),
                   pl.BlockSpec((1, t, LANES), lambda bi, i: (bi, i, 0))],
        out_shape=[jax.ShapeDtypeStruct((b, s, width), F32), jax.ShapeDtypeStruct((b, s, LANES), F32)],
        scratch_shapes=[pltpu.VMEM((t + 2 * HALO_ROWS, width), F32)],
        compiler_params=_cparams("parallel", "parallel"),
        name="gdn_prep",
    )(cqkv, cqkv, cqkv, conv_w, cba, par)


def _chunk_cumsum(x, rev):
    n = x.shape[0]
    row = lax.broadcasted_iota(jnp.int32, (n, 1), 0)
    sh = 1
    while sh < n:
        if rev:
            x = x + jnp.where(row < n - sh, pltpu.roll(x, n - sh, 0), 0.0)
        else:
            x = x + jnp.where(row >= sh, pltpu.roll(x, sh, 0), 0.0)
        sh *= 2
    return x


def _headnorm_gate(o, nw_ref, gate):
    outs = []
    for s in range(o.shape[1] // LANES):
        os_ = o[:, s * LANES:(s + 1) * LANES]
        ms = _segsum64(os_ * os_) * (1.0 / HEAD_DIM)
        outs.append(os_ * lax.rsqrt(ms + EPS) * nw_ref[...] * _silu(gate[:, s * LANES:(s + 1) * LANES]))
    return outs


def _gdn_kernel(rev, qkv_ref, bg_ref, *rest):
    if rev:
        of_ref, gate_ref, nw_ref, o_ref, state_ref, ob_ref = rest
    else:
        o_ref, state_ref = rest
        ob_ref = o_ref.at[0]
    t = qkv_ref.shape[1]

    @pl.when(pl.program_id(1) == 0)
    def _():
        state_ref[...] = jnp.zeros_like(state_ref)

    ri = lax.broadcasted_iota(jnp.int32, (CHUNK, CHUNK), 0)
    ci = lax.broadcasted_iota(jnp.int32, (CHUNK, CHUNK), 1)
    incl = (ci >= ri) if rev else (ci <= ri)
    strict = (ci > ri) if rev else (ci < ri)
    chunks = list(range(t // CHUNK))
    if rev:
        chunks.reverse()
    units = {}
    for c in chunks:
        r0 = c * CHUNK
        bg = bg_ref[0, r0:r0 + CHUNK, :]
        gc = _chunk_cumsum(bg, rev)
        gc_t = gc.T
        e_gc = jnp.exp(gc)
        g_last = gc[0:1, :] if rev else gc[CHUNK - 1:CHUNK, :]
        e_last = jnp.exp(g_last)
        e_rel = jnp.exp(g_last - gc)
        for h in range(C_HEADS):
            hb = h + (C_HEADS if rev else 0)
            hg = 2 * C_HEADS + hb
            q = qkv_ref[0, r0:r0 + CHUNK, h * HEAD_DIM:(h + 1) * HEAD_DIM]
            k = qkv_ref[0, r0:r0 + CHUNK, C_WIDTH + h * HEAD_DIM:C_WIDTH + (h + 1) * HEAD_DIM]
            v = qkv_ref[0, r0:r0 + CHUNK, 2 * C_WIDTH + h * HEAD_DIM:2 * C_WIDTH + (h + 1) * HEAD_DIM]
            beta = bg[:, hb:hb + 1]
            decay = jnp.exp(jnp.where(incl, gc[:, hg:hg + 1] - gc_t[hg:hg + 1, :], NEG))
            kb = k * beta
            kq = _dot_nt(_bf(jnp.concatenate([kb, q], axis=0)), _bf(k))
            units[c, h] = dict(
                m=jnp.where(strict, kq[:CHUNK] * decay, 0.0), qk=_bf(kq[CHUNK:] * decay),
                r=jnp.concatenate([v * beta, kb * e_gc[:, hg:hg + 1]], axis=1),
                q_in=_bf(q * e_gc[:, hg:hg + 1]), k_out=_bf(k * e_rel[:, hg:hg + 1]), e_last=e_last[:, hg:hg + 1])
    for un in units.values():
        un["p"] = _bf(un["m"])
        un["r"] = un["r"] - _dot(un["p"], _bf(un["r"]))
    for _ in range(5):
        for un in units.values():
            un["p"] = _bf(_dot(un["p"], un["p"]))
        for un in units.values():
            un["r"] = un["r"] + _dot(un["p"], _bf(un["r"]))
    for c in chunks:
        r0 = c * CHUNK
        states = [state_ref[h] for h in range(C_HEADS)]
        v_new = [units[c, h]["r"][:, :HEAD_DIM] - _dot(_bf(units[c, h]["r"][:, HEAD_DIM:]), _bf(states[h]))
                 for h in range(C_HEADS)]
        for h in range(C_HEADS):
            un = units[c, h]
            o = _dot(un["q_in"], _bf(states[h])) + _dot(un["qk"], _bf(v_new[h]))
            ob_ref[r0:r0 + CHUNK, h * HEAD_DIM:(h + 1) * HEAD_DIM] = o
            state_ref[h] = states[h] * un["e_last"] + _dot_tn(un["k_out"], _bf(v_new[h]))
    if rev:
        outs = _headnorm_gate(of_ref[0] + ob_ref[...], nw_ref, gate_ref[0])
        for s, val in enumerate(outs):
            o_ref[0, :, s * LANES:(s + 1) * LANES] = val.astype(o_ref.dtype)


def _gdn_scan(qkv, bg, rev, extra=(), t=512):
    b, s, width = qkv.shape
    n = s // t
    idx = (lambda bi, i: (bi, n - 1 - i, 0)) if rev else (lambda bi, i: (bi, i, 0))
    in_specs = [pl.BlockSpec((1, t, width), idx), pl.BlockSpec((1, t, LANES), idx)]
    scratch = [pltpu.VMEM((C_HEADS, HEAD_DIM, HEAD_DIM), F32)]
    if rev:
        in_specs += [pl.BlockSpec((1, t, C_WIDTH), idx), pl.BlockSpec((1, t, C_WIDTH), idx),
                     pl.BlockSpec((1, LANES), lambda bi, i: (0, 0))]
        scratch.append(pltpu.VMEM((t, C_WIDTH), F32))
    return pl.pallas_call(
        functools.partial(_gdn_kernel, rev),
        grid=(b, n),
        in_specs=in_specs,
        out_specs=pl.BlockSpec((1, t, C_WIDTH), idx),
        out_shape=jax.ShapeDtypeStruct((b, s, C_WIDTH), BF16 if rev else F32),
        scratch_shapes=scratch,
        compiler_params=_cparams("parallel", "arbitrary"),
        name="gdn_bwd" if rev else "gdn_fwd",
    )(qkv, bg, *extra)


def _gdn_mixer(cqkv, cba, gate, conv_w, a_log, dt_bias, norm_w):
    cw = jnp.zeros((HALO_ROWS, cqkv.shape[-1]), F32).at[:CONV_K].set(conv_w)
    par = jnp.zeros((HALO_ROWS, LANES), F32)
    par = par.at[0, 2 * C_HEADS:4 * C_HEADS].set(a_log.reshape(-1)).at[1, 2 * C_HEADS:4 * C_HEADS].set(dt_bias.reshape(-1))
    qkv, bg = _gdn_prep(cqkv, cba, cw, par)
    o_f = _gdn_scan(qkv, bg, False)
    return _gdn_scan(qkv, bg, True, (o_f, gate, jnp.tile(norm_w, 2)[None]))


SUB = 16


def _segment_cumsum(x, seg, rev):
    n = x.shape[0]
    pos = lax.broadcasted_iota(jnp.int32, (n, 1), 0) % seg
    sh = 1
    while sh < seg:
        if rev:
            x = x + jnp.where(pos < seg - sh, pltpu.roll(x, n - sh, 0), 0.0)
        else:
            x = x + jnp.where(pos >= sh, pltpu.roll(x, sh, 0), 0.0)
        sh *= 2
    return x


def _hgrn_kernel(rev, q_ref, v_ref, f_ref, lb_ref, *rest):
    if rev:
        of_ref, gate_ref, nw_ref, o_ref, state_ref, b_ref, k_ref, ob_ref = rest
    else:
        o_ref, state_ref, b_ref, k_ref = rest
        ob_ref = o_ref.at[0]
    t = q_ref.shape[1]
    nsub = t // SUB
    npair = D_WIDTH // LANES

    @pl.when(pl.program_id(1) == 0)
    def _():
        state_ref[...] = jnp.zeros_like(state_ref)

    fl = f_ref[0]
    log_lb, log_1m_lb, one_m_lb = lb_ref[0:1, :], lb_ref[1:2, :], lb_ref[2:3, :]
    y = log_1m_lb + (jnp.minimum(fl, 0.0) - jnp.log1p(jnp.exp(-jnp.abs(fl))))
    mx = jnp.maximum(log_lb, y)
    logf = mx + jnp.log(jnp.exp(log_lb - mx) + jnp.exp(y - mx))
    b_ref[...] = _segment_cumsum(logf, SUB, rev)
    k_ref[...] = one_m_lb * _sigmoid(-fl)

    bd_f = _head_blockdiag(F32)
    bd = _bf(bd_f)
    row = lax.broadcasted_iota(jnp.int32, (SUB, 1), 0)

    def sub_block(it, carry):
        r0 = pl.multiple_of((nsub - 1 - it if rev else it) * SUB, SUB)
        for p in range(npair):
            cols = slice(p * LANES, (p + 1) * LANES)
            b = b_ref[pl.ds(r0, SUB), cols]
            q = q_ref[0, pl.ds(r0, SUB), cols]
            k = k_ref[pl.ds(r0, SUB), cols]
            v = v_ref[0, pl.ds(r0, SUB), cols]
            b_end = b[0:1, :] if rev else b[SUB - 1:SUB, :]
            state_t = state_ref[p]
            o = _dot_nt(_bf(q * jnp.exp(b)), _bf(state_t))
            xs = []
            for j in range(SUB):
                mask = (row <= j) if rev else (row >= j)
                e = jnp.exp(jnp.where(mask, b - b[j:j + 1, :], NEG))
                xs.append(_bf(q * e * k[j:j + 1, :]))
            ys = _dot(jnp.concatenate(xs, axis=0), bd)
            for j in range(SUB):
                o = o + ys[j * SUB:(j + 1) * SUB] * v[j:j + 1, :]
            ob_ref[pl.ds(r0, SUB), cols] = o
            upd = _dot_tn(_bf(v), _bf(k * jnp.exp(b_end - b)))
            state_ref[p] = state_t * jnp.exp(b_end) + upd * bd_f
        return carry

    lax.fori_loop(0, nsub, sub_block, 0, unroll=True)
    if rev:
        outs = _headnorm_gate(of_ref[0] + ob_ref[...], nw_ref, gate_ref[0])
        for s, val in enumerate(outs):
            o_ref[0, :, s * LANES:(s + 1) * LANES] = val.astype(o_ref.dtype)


def _hgrn_scan(q, v, f, lbp, rev, extra=(), t=256):
    b, s, width = q.shape
    n = s // t
    d = 1 if rev else 0
    idx = (lambda bi, i: (bi, n - 1 - i, 0)) if rev else (lambda bi, i: (bi, i, 0))
    fidx = (lambda bi, i: (bi, n - 1 - i, 1)) if rev else (lambda bi, i: (bi, i, 0))
    in_specs = [pl.BlockSpec((1, t, width), idx), pl.BlockSpec((1, t, width), idx), pl.BlockSpec((1, t, width), fidx),
                pl.BlockSpec((HALO_ROWS, width), lambda bi, i: (0, d))]
    scratch = [pltpu.VMEM((width // LANES, LANES, LANES), F32), pltpu.VMEM((t, width), F32), pltpu.VMEM((t, width), F32)]
    if rev:
        in_specs += [pl.BlockSpec((1, t, width), idx), pl.BlockSpec((1, t, width), idx),
                     pl.BlockSpec((1, LANES), lambda bi, i: (0, 0))]
        scratch.append(pltpu.VMEM((t, width), F32))
    return pl.pallas_call(
        functools.partial(_hgrn_kernel, rev),
        grid=(b, n),
        in_specs=in_specs,
        out_specs=pl.BlockSpec((1, t, width), idx),
        out_shape=jax.ShapeDtypeStruct((b, s, width), BF16 if rev else F32),
        scratch_shapes=scratch,
        compiler_params=_cparams("parallel", "arbitrary"),
        name="hgrn_bwd" if rev else "hgrn_fwd",
    )(q, v, f, lbp, *extra)


def _hgrn_mixer(q, v, f, gate, lb, norm_w):
    lbp = jnp.zeros((HALO_ROWS, 2 * D_WIDTH), F32).at[0].set(jnp.log(lb)).at[1].set(jnp.log1p(-lb)).at[2].set(1.0 - lb)
    o_f = _hgrn_scan(q, v, f, lbp, False)
    return _hgrn_scan(q, v, f, lbp, True, (o_f, gate, jnp.tile(norm_w, 2)[None]))


def _hgrn_lower_bound(d_lb, layer):
    lb_cum = jnp.cumsum(jax.nn.softmax(d_lb.astype(F32), axis=0), axis=0)
    return (lb_cum[layer] - lb_cum[0]).reshape(-1)


def _out_proj_kernel(x_ref, oa_ref, ob0_ref, l0_ref, ob1_ref, l1_ref, ob2_ref, l2_ref, oc_ref, od_ref,
                     wa_ref, wb_ref, wc_ref, wd_ref, y_ref):
    lses = [l0_ref[...], l1_ref[...], l2_ref[...]]
    m = jnp.maximum(jnp.maximum(lses[0], lses[1]), lses[2])
    ws = [jnp.exp(l - m) for l in lses]
    ob = (ws[0] * ob0_ref[...] + ws[1] * ob1_ref[...] + ws[2] * ob2_ref[...]) / (ws[0] + ws[1] + ws[2])
    y = x_ref[...] + _dot(oa_ref[...], wa_ref[...]) + _dot(_bf(ob), wb_ref[...])
    y_ref[...] = y + _dot(oc_ref[...], wc_ref[...]) + _dot(od_ref[...], wd_ref[...])


def _out_proj(x2, oa, ob_lse, oc, od, ws, tm=512):
    n = x2.shape[0]
    acts = [x2, oa] + [a for pair in ob_lse for a in pair] + [oc, od]
    return pl.pallas_call(
        _out_proj_kernel,
        grid=(n // tm,),
        in_specs=[_row_spec(tm, a.shape[1]) for a in acts] + [_const_spec(w.shape) for w in ws],
        out_specs=_row_spec(tm, D_MODEL),
        out_shape=jax.ShapeDtypeStruct((n, D_MODEL), F32),
        compiler_params=_cparams("parallel"),
        name="out_proj",
    )(*acts, *ws)


I_PER_STEP = 8
PEER_SQRT_HALF = 0.7071067811865476


def _top_values(x, count):
    vals = []
    for _ in range(count):
        m = jnp.max(x, axis=0, keepdims=True)
        vals.append(m)
        x = jnp.where(x == m, NEG, x)
    return vals


def _peer_stats(x_ref, nw_ref, wq_ref, keys_ref, h_ref, e1_ref, theta_ref, a_ref):
    h = _bf(_rmsnorm_rows(x_ref[...], nw_ref[...]))
    h_ref[...] = h
    q = _dot(h, wq_ref[...])
    ranks = PEER_TOPK + 1
    for hd in range(PEER_HEADS):
        s_t = []
        for p in range(2):
            c0 = (2 * hd + p) * PEER_HALF
            s_t.append(_dot_nt(keys_ref[hd, p], _bf(q[:, c0:c0 + PEER_HALF])))
        top = [_top_values(s, ranks) for s in s_t]
        top1 = jnp.concatenate(top[1], axis=0)
        cand = jnp.concatenate([top[0][i - 1] + top1[:ranks // i] for i in range(1, ranks + 1)], axis=0)
        best = _top_values(cand, ranks)
        z = sum(jnp.exp(b - best[0]) for b in best[:PEER_TOPK])
        thr = 0.5 * (best[PEER_TOPK - 1] + best[PEER_TOPK])
        e1_ref[hd] = jnp.exp(s_t[1] - top[1][0])
        theta_ref[hd] = jnp.exp((thr - top[1][0]) - s_t[0])
        a_ref[hd] = jnp.exp(s_t[0] - top[0][0]) * (0.5 / z)


def _peer_weights(jj, u_ref, h_ref, e1_ref, theta_ref, a_ref, w_ref):
    act = _dot_nt(u_ref[...], h_ref[...])
    for ii in range(I_PER_STEP):
        i = jj * I_PER_STEP + ii
        g = None
        for hd in range(PEER_HEADS):
            e1 = e1_ref[hd]
            term = jnp.where(e1 >= theta_ref[hd, pl.ds(i, 1), :], e1, 0.0) * a_ref[hd, pl.ds(i, 1), :]
            g = term if g is None else g + term
        a_ii = act[ii * N_KEYS:(ii + 1) * N_KEYS]
        w_ref[ii * N_KEYS:(ii + 1) * N_KEYS, :] = _bf(g * (a_ii + a_ii * lax.erf(a_ii * PEER_SQRT_HALF)))


def _peer_kernel(final, x_ref, nw_ref, wq_ref, keys_ref, u_ref, vt_ref, fnw_ref, y_ref,
                 h_ref, e1_ref, theta_ref, a_ref, w_ref, acc_ref):
    j = pl.program_id(1)
    last = pl.num_programs(1) - 1
    stats = (e1_ref, theta_ref, a_ref)

    @pl.when(j == 0)
    def _():
        _peer_stats(x_ref, nw_ref, wq_ref, keys_ref, h_ref, *stats)
        _peer_weights(0, u_ref, h_ref, *stats, w_ref.at[0])

    @pl.when(j == 1)
    def _():
        acc_ref[...] = _dot(vt_ref[...], w_ref[0])
        _peer_weights(1, u_ref, h_ref, *stats, w_ref.at[1])

    @pl.when((j > 1) & (j < last))
    def _():
        acc_ref[...] += _dot(vt_ref[...], w_ref[(j - 1) % 2])
        _peer_weights(j, u_ref, h_ref, *stats, w_ref.at[j % 2])

    @pl.when(j == last)
    def _():
        y = x_ref[...] + (acc_ref[...] + _dot(vt_ref[...], w_ref[(last - 1) % 2])).T
        if final:
            y = _rmsnorm_rows(y, fnw_ref[...])
        y_ref[...] = y


def _peer(x2, nw, wq, keys, u, vt, fnw, final, tm=512):
    n = x2.shape[0]
    ec = I_PER_STEP * N_KEYS
    nblk = N_EXPERTS // ec
    tok = lambda i, j: (i, 0)
    const2 = lambda i, j: (0, 0)
    stat = pltpu.VMEM((PEER_HEADS, N_KEYS, tm), F32)
    return pl.pallas_call(
        functools.partial(_peer_kernel, final),
        grid=(n // tm, nblk + 1),
        in_specs=[pl.BlockSpec((tm, D_MODEL), tok), pl.BlockSpec((1, D_MODEL), const2),
                  pl.BlockSpec(wq.shape, const2), pl.BlockSpec(keys.shape, lambda i, j: (0, 0, 0, 0)),
                  pl.BlockSpec((ec, D_MODEL), lambda i, j: (jnp.minimum(j, nblk - 1), 0)),
                  pl.BlockSpec((D_MODEL, ec), lambda i, j: (0, jnp.maximum(j - 1, 0))),
                  pl.BlockSpec((1, D_MODEL), const2)],
        out_specs=pl.BlockSpec((tm, D_MODEL), tok),
        out_shape=jax.ShapeDtypeStruct((n, D_MODEL), F32),
        scratch_shapes=[pltpu.VMEM((tm, D_MODEL), BF16), stat, stat, stat, pltpu.VMEM((2, ec, tm), BF16),
                        pltpu.VMEM((D_MODEL, tm), F32)],
        compiler_params=_cparams("parallel", "arbitrary"),
        name="peer",
    )(x2, nw, wq, keys, u, vt, fnw)


def _split_cols(w):
    parts, off = [], 0
    for width in IN_SPLITS:
        parts.append(w[:, off:off + width])
        off += width
    return parts


def _axial_tables(seq):
    t = jnp.arange(seq)
    half = HEAD_DIM // 4
    inv = AXIAL_THETA ** (-jnp.arange(half, dtype=F32) / half)
    ang_r = (t // GRID_W).astype(F32)[:, None] * inv[None, :]
    ang_c = (t % GRID_W).astype(F32)[:, None] * inv[None, :]
    zero = jnp.zeros_like(ang_r)
    cos = jnp.concatenate([jnp.cos(ang_r), jnp.cos(ang_r), jnp.cos(ang_c), jnp.cos(ang_c)], axis=1)
    sneg = jnp.concatenate([-jnp.sin(ang_r), zero, -jnp.sin(ang_c), zero], axis=1)
    spos = jnp.concatenate([zero, jnp.sin(ang_r), zero, jnp.sin(ang_c)], axis=1)
    return tuple(jnp.tile(a, (1, 2)) for a in (cos, sneg, spos))


def _rope_tables(seq):
    half = ROT_DIMS // 2
    inv = ROPE_THETA ** (-jnp.arange(half, dtype=F32) / half)
    ang = jnp.arange(seq).astype(F32)[:, None] * inv[None, :]
    rest = HEAD_DIM - ROT_DIMS
    cos = jnp.concatenate([jnp.cos(ang), jnp.cos(ang), jnp.ones((seq, rest), F32)], axis=1)
    sneg = jnp.concatenate([-jnp.sin(ang), jnp.zeros((seq, half + rest), F32)], axis=1)
    spos = jnp.concatenate([jnp.zeros((seq, half), F32), jnp.sin(ang), jnp.zeros((seq, rest), F32)], axis=1)
    return tuple(jnp.tile(a, (1, 2)) for a in (cos, sneg, spos))


def _layout_w_in(w):
    (a_q, a_k, a_v, b_q, b_k, b_v, c_qkv, c_beta, c_a, c_gate, d_q, d_i, d_f, d_gate) = _split_cols(w)
    zero = jnp.zeros((D_MODEL, HEAD_DIM), w.dtype)
    q_slabs = []
    for h in range(A_Q_HEADS):
        qh = a_q[:, h * HEAD_DIM:(h + 1) * HEAD_DIM]
        q_slabs += [qh, zero] if h // A_GROUP == 0 else [zero, qh]
    wa = jnp.concatenate(q_slabs + [a_k, a_v], axis=1)
    b_cols = []
    for g in range(len(DIL_CONFIGS)):
        b_cols += [t[:, g * LANES:(g + 1) * LANES] for t in (b_q, b_k, b_v)]
    wb = jnp.concatenate(b_cols, axis=1)
    pad = jnp.zeros((D_MODEL, LANES - 4 * C_HEADS), w.dtype)
    wc = jnp.concatenate([c_qkv, c_gate, c_beta, c_a, pad], axis=1)
    wd = jnp.concatenate([d_q, d_i, d_f, d_gate], axis=1)
    return tuple(_bf(t) for t in (wa, wb, wc, wd))


def _layout_w_out(w):
    bounds = (0, A_WIDTH, A_WIDTH + LANES, A_WIDTH + LANES + C_WIDTH, A_WIDTH + LANES + C_WIDTH + D_WIDTH)
    return tuple(_bf(w[lo:hi]) for lo, hi in zip(bounds[:-1], bounds[1:]))


def _layer_params(l, norm1_w, w_in, a_qnorm_w, a_knorm_w, c_conv_w, c_a_log, c_dt_bias, c_norm_w, d_lb, d_norm_w, w_out,
                  norm2_w, peer_w_query, peer_sub_keys, peer_u, peer_v):
    return dict(
        norm1=norm1_w[l][None], w_in=_layout_w_in(w_in[l]),
        qnw=jnp.tile(a_qnorm_w[l], 2)[None], knw=jnp.tile(a_knorm_w[l], 2)[None],
        conv_w=c_conv_w[l], a_log=c_a_log[l], dt_bias=c_dt_bias[l], c_norm=c_norm_w[l],
        lb=_hgrn_lower_bound(d_lb, l), d_norm=d_norm_w[l], w_out=_layout_w_out(w_out[l]),
        norm2=norm2_w[l][None], wq=_bf(peer_w_query[l]), keys=_bf(peer_sub_keys[l]),
        u=_bf(peer_u[l]), vt=_bf(peer_v[l].T))


def _trunk(x, layers, final_norm_w):
    b, s, _ = x.shape
    n = b * s
    axial, rope = _axial_tables(s), _rope_tables(s)
    x2 = x.reshape(n, D_MODEL)
    seq = lambda t: t.reshape(b, s, t.shape[-1])
    flat = lambda t: t.reshape(n, t.shape[-1])
    for l, p in enumerate(layers):
        qa, ka, va, zb, cqkv, cgate, cba, dq, di, df, dgate = _proj(
            x2, p["norm1"], p["w_in"], p["qnw"], p["knw"], axial, rope, s)
        o_a = _attn_a(seq(qa), seq(ka), seq(va))
        ob_lse = [tuple(flat(t) for t in _attn_b(seq(zb), g)) for g in range(len(DIL_CONFIGS))]
        o_c = _gdn_mixer(seq(cqkv), seq(cba), seq(cgate), p["conv_w"], p["a_log"], p["dt_bias"], p["c_norm"])
        o_d = _hgrn_mixer(seq(dq), seq(di), seq(df), seq(dgate), p["lb"], p["d_norm"])
        x2 = _out_proj(x2, flat(o_a), ob_lse, flat(o_c), flat(o_d), p["w_out"])
        x2 = _peer(x2, p["norm2"], p["wq"], p["keys"], p["u"], p["vt"], final_norm_w[None], l == len(layers) - 1)
    return x2.reshape(b, s, D_MODEL)


def kernel(x_prompt, x_sample, norm1_w, w_in, a_qnorm_w, a_knorm_w, c_conv_w, c_a_log, c_dt_bias, c_norm_w, d_lb, d_norm_w, w_out, norm2_w, peer_w_query, peer_sub_keys, peer_u, peer_v, final_norm_w):
    layers = [_layer_params(l, norm1_w, w_in, a_qnorm_w, a_knorm_w, c_conv_w, c_a_log, c_dt_bias, c_norm_w, d_lb, d_norm_w,
                            w_out, norm2_w, peer_w_query, peer_sub_keys, peer_u, peer_v) for l in range(DEPTH)]
    return (_trunk(x_prompt, layers, final_norm_w), _trunk(x_sample, layers, final_norm_w))
```
